```python
import math
import jax, jax.numpy as jnp
from jax import lax
import numpy as np

D_MODEL = 1024
BATCH = 8
SEQ = 4096
DEPTH = 4

DA_HEADS = 4
DA_QK_DIM = 64
DA_V_DIM = 2 * DA_QK_DIM
DA_Q_W = DA_HEADS * 2 * DA_QK_DIM
DA_V_W = DA_HEADS * DA_V_DIM
Q_BLOCK = 128
ML_HEADS = 4
ML_QK_DIM = 64
ML_V_DIM = 128
ML_QK_W = ML_HEADS * ML_QK_DIM
ML_V_W = ML_HEADS * ML_V_DIM
CONV_WIDTH = 4
CHUNK = 64
MIX_W = DA_V_W + ML_V_W
IN_SPLITS = (DA_Q_W, DA_Q_W, DA_V_W, 2 * ML_QK_W, ML_V_W, ML_V_W)
IN_W = sum(IN_SPLITS) + 2 * ML_HEADS
N_EXPERTS = 32
TOP_K = 4
D_FF = 1024
SWIGLU_LIMIT = 7.0
SWIGLU_ALPHA = 1.702
MOE_BLOCK = 128
DN_ALPHA = (2 * DEPTH) ** 0.25
DN_BETA = (8 * DEPTH) ** -0.25
LN_EPS = 1e-5
RMS_EPS = 1e-5

kernel_name = "hybrid_diffattn_mlstm_moe_deepnorm"


def layer_norm(x, g, b):
    xf = x.astype(jnp.float32)
    mu = jnp.mean(xf, -1, keepdims=True)
    var = jnp.mean(jnp.square(xf - mu), -1, keepdims=True)
    return ((xf - mu) * lax.rsqrt(var + LN_EPS) * g + b).astype(x.dtype)


def rms_norm(x, g):
    xf = x.astype(jnp.float32)
    return (xf * lax.rsqrt(jnp.mean(xf * xf, -1, keepdims=True) + RMS_EPS) * g).astype(x.dtype)


def causal_depthwise_conv(x, w, b):
    y = lax.conv_general_dilated(x, w[:, None, :].astype(x.dtype), window_strides=(1,),
                                 padding=[(CONV_WIDTH - 1, 0)],
                                 dimension_numbers=('NWC', 'WIO', 'NWC'),
                                 feature_group_count=x.shape[-1])
    return y + b


def diff_attention(q, k, v, lam, lam_init, norm_g):
    B, S = q.shape[:2]
    nq = S // Q_BLOCK
    scale = DA_QK_DIM ** -0.5
    q_blocks = jnp.moveaxis(q.reshape(B, nq, Q_BLOCK, DA_HEADS, 2, DA_QK_DIM), 1, 0)
    k_pos = jnp.arange(S)

    def one_block(args):
        qb, blk = args
        s = jnp.einsum('bqhcd,bkhcd->bhcqk', qb, k).astype(jnp.float32) * scale
        q_pos = blk * Q_BLOCK + jnp.arange(Q_BLOCK)
        s = jnp.where(q_pos[:, None] >= k_pos[None, :], s, -jnp.inf)
        p = jax.nn.softmax(s, axis=-1)
        a = p[:, :, 0] - lam * p[:, :, 1]
        return jnp.einsum('bhqk,bkhe->bqhe', a.astype(v.dtype), v)

    o = lax.map(one_block, (q_blocks, jnp.arange(nq)))
    o = jnp.moveaxis(o, 0, 1).reshape(B, S, DA_HEADS, DA_V_DIM)
    o = rms_norm(o, norm_g) * (1.0 - lam_init)
    return o.reshape(B, S, DA_V_W)


def mlstm_chunkwise(q, k, v, i_pre, f_pre):
    B, S = q.shape[:2]
    nc = S // CHUNK
    f32 = jnp.float32

    def chunks(t):
        t = t.reshape((B, nc, CHUNK, ML_HEADS) + t.shape[3:])
        return jnp.moveaxis(t, 3, 1)

    q = chunks(q).astype(f32) * (ML_QK_DIM ** -0.5)
    k = chunks(k).astype(f32)
    v = chunks(v).astype(f32)
    ig = chunks(i_pre).astype(f32)
    lf = jax.nn.log_sigmoid(chunks(f_pre).astype(f32))
    bcum = jnp.cumsum(lf, axis=-1)
    g = bcum[..., -1]

    w_end = g[..., None] - bcum + ig
    m_loc = jnp.max(w_end, -1)
    e_end = jnp.exp(w_end - m_loc[..., None])
    C_loc = jnp.einsum('bhcs,bhcsv,bhcsk->bhcvk', e_end, v, k)
    n_loc = jnp.einsum('bhcs,bhcsk->bhck', e_end, k)

    def step(carry, xs):
        C, n, m = carry
        g_c, m_l, C_l, n_l = xs
        m_new = jnp.maximum(g_c + m, m_l)
        a = jnp.exp(g_c + m - m_new)
        bb = jnp.exp(m_l - m_new)
        C_new = a[..., None, None] * C + bb[..., None, None] * C_l
        n_new = a[..., None] * n + bb[..., None] * n_l
        return (C_new, n_new, m_new), (C, n, m)

    init = (jnp.zeros((B, ML_HEADS, ML_V_DIM, ML_QK_DIM), f32),
            jnp.zeros((B, ML_HEADS, ML_QK_DIM), f32),
            jnp.full((B, ML_HEADS), -jnp.inf, f32))
    xs = (jnp.moveaxis(g, 2, 0), jnp.moveaxis(m_loc, 2, 0),
          jnp.moveaxis(C_loc, 2, 0), jnp.moveaxis(n_loc, 2, 0))
    _, (C_prev, n_prev, m_prev) = lax.scan(step, init, xs)
    C_prev = jnp.moveaxis(C_prev, 0, 2)
    n_prev = jnp.moveaxis(n_prev, 0, 2)
    m_prev = jnp.moveaxis(m_prev, 0, 2)

    causal = jnp.tril(jnp.ones((CHUNK, CHUNK), bool))
    D = jnp.where(causal, bcum[..., :, None] - bcum[..., None, :] + ig[..., None, :], -jnp.inf)
    inter_log = bcum + m_prev[..., None]
    m_t = jnp.maximum(inter_log, jnp.max(D, -1))
    d_exp = jnp.exp(D - m_t[..., None])
    inter_w = jnp.exp(inter_log - m_t)
    s = jnp.einsum('bhctk,bhcsk->bhcts', q, k) * d_exp
    num = (inter_w[..., None] * jnp.einsum('bhcvk,bhctk->bhctv', C_prev, q)
           + jnp.einsum('bhcts,bhcsv->bhctv', s, v))
    den = inter_w * jnp.einsum('bhck,bhctk->bhct', n_prev, q) + jnp.sum(s, -1)
    h = num / jnp.maximum(jnp.abs(den), jnp.exp(-m_t))[..., None]
    return jnp.moveaxis(h, 1, 3).reshape(B, S, ML_HEADS, ML_V_DIM)


def mixer(x, w_in, conv_w, conv_b, gate_b, lam_q1, lam_k1, lam_q2, lam_k2, da_norm_g, w_out, lam_init):
    B, S, _ = x.shape
    proj = x @ w_in
    da_q, da_k, da_v, ml_qk, ml_v, ml_o, ml_g = jnp.split(proj, list(np.cumsum(IN_SPLITS)), axis=-1)
    lam = (jnp.exp(jnp.sum(lam_q1 * lam_k1).astype(jnp.float32))
           - jnp.exp(jnp.sum(lam_q2 * lam_k2).astype(jnp.float32)) + lam_init)
    da_out = diff_attention(da_q.reshape(B, S, DA_HEADS, 2, DA_QK_DIM),
                            da_k.reshape(B, S, DA_HEADS, 2, DA_QK_DIM),
                            da_v.reshape(B, S, DA_HEADS, DA_V_DIM), lam, lam_init, da_norm_g)
    ml_qk = jax.nn.silu(causal_depthwise_conv(ml_qk, conv_w, conv_b))
    ml_q, ml_k = jnp.split(ml_qk, 2, axis=-1)
    gates = ml_g + gate_b
    h = mlstm_chunkwise(ml_q.reshape(B, S, ML_HEADS, ML_QK_DIM),
                        ml_k.reshape(B, S, ML_HEADS, ML_QK_DIM),
                        ml_v.reshape(B, S, ML_HEADS, ML_V_DIM),
                        gates[..., :ML_HEADS], gates[..., ML_HEADS:])
    ml_out = (h.reshape(B, S, ML_V_W) * jax.nn.sigmoid(ml_o.astype(jnp.float32))).astype(x.dtype)
    return jnp.concatenate([da_out, ml_out], axis=-1) @ w_out


def moe(x, w_router, b_router, w_gu, b_gu, w_down, b_down):
    B, S, D = x.shape
    N = B * S
    A = N * TOP_K
    xt = x.reshape(N, D)
    logits = (xt @ w_router + b_router).astype(jnp.float32)
    top_v, top_e = lax.top_k(logits, TOP_K)
    gates = jax.nn.softmax(top_v, axis=-1)
    e_flat = top_e.reshape(A)
    tok_flat = jnp.arange(A, dtype=jnp.int32) // TOP_K
    order = jnp.argsort(e_flat)
    e_sorted = e_flat[order]
    tok_sorted = tok_flat[order]
    gate_sorted = gates.reshape(A)[order]
    counts = jnp.zeros((N_EXPERTS,), jnp.int32).at[e_flat].add(1)
    starts = jnp.cumsum(counts) - counts
    padded = (counts + MOE_BLOCK - 1) // MOE_BLOCK * MOE_BLOCK
    pad_ends = jnp.cumsum(padded)
    pad_starts = pad_ends - padded
    dest = pad_starts[e_sorted] + jnp.arange(A, dtype=jnp.int32) - starts[e_sorted]
    n_rows = A + N_EXPERTS * MOE_BLOCK
    n_blocks = n_rows // MOE_BLOCK
    row_tok = jnp.full((n_rows,), N, jnp.int32).at[dest].set(tok_sorted)
    x_pad = jnp.concatenate([xt, jnp.zeros((1, D), xt.dtype)], axis=0)
    xs = x_pad[row_tok].reshape(n_blocks, MOE_BLOCK, D)
    block_e = jnp.minimum(jnp.searchsorted(pad_ends, jnp.arange(n_blocks) * MOE_BLOCK, side='right'),
                          N_EXPERTS - 1)

    def expert_block(args):
        xb, e = args
        hgu = xb @ w_gu[e] + b_gu[e]
        gate = jnp.minimum(hgu[:, :D_FF], SWIGLU_LIMIT)
        up = jnp.clip(hgu[:, D_FF:], -SWIGLU_LIMIT, SWIGLU_LIMIT)
        glu = gate * jax.nn.sigmoid(SWIGLU_ALPHA * gate)
        return ((up + 1.0) * glu) @ w_down[e] + b_down[e]

    ys = lax.map(expert_block, (xs, block_e)).reshape(n_rows, D)
    y = jax.ops.segment_sum(ys[dest] * gate_sorted[:, None].astype(ys.dtype), tok_sorted, num_segments=N)
    return y.reshape(B, S, D)


def setup_inputs(seed: int = 0) -> dict:
    key = jax.random.key(seed)
    ks = jax.random.split(key, 22)
    nrm = jax.random.normal
    f32 = jnp.float32
    x = nrm(ks[0], (BATCH, SEQ, D_MODEL), f32)
    col_scale = np.ones((IN_W,), np.float32)
    off_v = 2 * DA_Q_W
    col_scale[off_v:off_v + DA_V_W] = DN_BETA
    off_mv = 2 * DA_Q_W + DA_V_W + 2 * ML_QK_W
    col_scale[off_mv:off_mv + ML_V_W] = DN_BETA
    w_in = nrm(ks[1], (DEPTH, D_MODEL, IN_W), f32) * (D_MODEL ** -0.5) * jnp.asarray(col_scale)
    conv_w = nrm(ks[2], (DEPTH, CONV_WIDTH, 2 * ML_QK_W), f32) * (CONV_WIDTH ** -0.5)
    conv_b = 0.01 * nrm(ks[3], (DEPTH, 2 * ML_QK_W), f32)
    gate_b = jnp.concatenate([
        0.1 * nrm(ks[4], (DEPTH, ML_HEADS), f32),
        jnp.linspace(3.0, 6.0, ML_HEADS, dtype=f32)[None, :] + 0.1 * nrm(ks[5], (DEPTH, ML_HEADS), f32)], axis=-1)
    lam_q1 = 0.1 * nrm(ks[6], (DEPTH, DA_QK_DIM), f32)
    lam_k1 = 0.1 * nrm(ks[7], (DEPTH, DA_QK_DIM), f32)
    lam_q2 = 0.1 * nrm(ks[8], (DEPTH, DA_QK_DIM), f32)
    lam_k2 = 0.1 * nrm(ks[9], (DEPTH, DA_QK_DIM), f32)
    da_norm_g = 1.0 + 0.01 * nrm(ks[10], (DEPTH, DA_V_DIM), f32)
    w_out = nrm(ks[11], (DEPTH, MIX_W, D_MODEL), f32) * (MIX_W ** -0.5) * DN_BETA
    ln1_g = 1.0 + 0.01 * nrm(ks[12], (DEPTH, D_MODEL), f32)
    ln1_b = 0.01 * nrm(ks[13], (DEPTH, D_MODEL), f32)
    w_router = nrm(ks[14], (DEPTH, D_MODEL, N_EXPERTS), f32) * (D_MODEL ** -0.5)
    b_router = 0.01 * nrm(ks[15], (DEPTH, N_EXPERTS), f32)
    w_gu = nrm(ks[16], (DEPTH, N_EXPERTS, D_MODEL, 2 * D_FF), f32) * (D_MODEL ** -0.5) * DN_BETA
    b_gu = 0.01 * nrm(ks[17], (DEPTH, N_EXPERTS, 2 * D_FF), f32)
    w_down = nrm(ks[18], (DEPTH, N_EXPERTS, D_FF, D_MODEL), f32) * (D_FF ** -0.5) * DN_BETA
    b_down = 0.01 * nrm(ks[19], (DEPTH, N_EXPERTS, D_MODEL), f32)
    ln2_g = 1.0 + 0.01 * nrm(ks[20], (DEPTH, D_MODEL), f32)
    ln2_b = 0.01 * nrm(ks[21], (DEPTH, D_MODEL), f32)
    return {"x": x, "w_in": w_in, "conv_w": conv_w, "conv_b": conv_b, "gate_b": gate_b,
            "lam_q1": lam_q1, "lam_k1": lam_k1, "lam_q2": lam_q2, "lam_k2": lam_k2,
            "da_norm_g": da_norm_g, "w_out": w_out, "ln1_g": ln1_g, "ln1_b": ln1_b,
            "w_router": w_router, "b_router": b_router, "w_gu": w_gu, "b_gu": b_gu,
            "w_down": w_down, "b_down": b_down, "ln2_g": ln2_g, "ln2_b": ln2_b}


def reference(x, w_in, conv_w, conv_b, gate_b, lam_q1, lam_k1, lam_q2, lam_k2, da_norm_g, w_out,
              ln1_g, ln1_b, w_router, b_router, w_gu, b_gu, w_down, b_down, ln2_g, ln2_b):
    for l in range(DEPTH):
        lam_init = 0.8 - 0.6 * math.exp(-0.3 * l)
        mix = mixer(x, w_in[l], conv_w[l], conv_b[l], gate_b[l], lam_q1[l], lam_k1[l], lam_q2[l],
                    lam_k2[l], da_norm_g[l], w_out[l], lam_init)
        x = layer_norm(DN_ALPHA * x + mix, ln1_g[l], ln1_b[l])
        ffn = moe(x, w_router[l], b_router[l], w_gu[l], b_gu[l], w_down[l], b_down[l])
        x = layer_norm(DN_ALPHA * x + ffn, ln2_g[l], ln2_b[l])
    return x
```

```python
import functools
import math

import jax
import jax.numpy as jnp
from jax import lax
from jax.experimental import pallas as pl
from jax.experimental.pallas import tpu as pltpu

F32 = jnp.float32
BF16 = jnp.bfloat16
I32 = jnp.int32

D_MODEL = 1024
DEPTH = 4
DA_HEADS = 4
DA_QK_DIM = 64
DA_V_DIM = 128
ML_HEADS = 4
ML_QK_DIM = 64
ML_V_DIM = 128
CONV_WIDTH = 4
N_EXPERTS = 32
TOP_K = 4
D_FF = 1024
SWIGLU_LIMIT = 7.0
SWIGLU_ALPHA = 1.702
DN_ALPHA = (2 * DEPTH) ** 0.25
LN_EPS = 1e-5
RMS_EPS = 1e-5

LANES = 128
SEC = 512
GATE_W = 2 * LANES
IN_W_PAD = 6 * SEC + GATE_W
VMEM_LIMIT = 56 * 1024 * 1024
NEG_INF = float("-inf")


def _cparams(sem):
    return pltpu.CompilerParams(dimension_semantics=sem, vmem_limit_bytes=VMEM_LIMIT)


def _inproj_kernel(x_ref, w_ref, qkv_ref, ml_ref, gate_ref):
    xb = x_ref[...].astype(BF16)
    for c in range(3):
        sl = slice(c * SEC, (c + 1) * SEC)
        qkv_ref[:, sl] = jnp.dot(xb, w_ref[:, sl], preferred_element_type=F32).astype(BF16)
    for c in range(3):
        ml_ref[:, c * SEC:(c + 1) * SEC] = jnp.dot(
            xb, w_ref[:, (3 + c) * SEC:(4 + c) * SEC], preferred_element_type=F32)
    gate_ref[...] = jnp.dot(xb, w_ref[:, 6 * SEC:], preferred_element_type=F32)


def _inproj(x, w, tm):
    n = x.shape[0]
    return pl.pallas_call(
        _inproj_kernel,
        grid=(n // tm,),
        in_specs=[pl.BlockSpec((tm, D_MODEL), lambda i: (i, 0)),
                  pl.BlockSpec((D_MODEL, IN_W_PAD), lambda i: (0, 0))],
        out_specs=[pl.BlockSpec((tm, 3 * SEC), lambda i: (i, 0)),
                   pl.BlockSpec((tm, 3 * SEC), lambda i: (i, 0)),
                   pl.BlockSpec((tm, GATE_W), lambda i: (i, 0))],
        out_shape=[jax.ShapeDtypeStruct((n, 3 * SEC), BF16),
                   jax.ShapeDtypeStruct((n, 3 * SEC), F32),
                   jax.ShapeDtypeStruct((n, GATE_W), F32)],
        compiler_params=_cparams(("parallel",)),
        name="in_proj",
    )(x, w)


def _attn_kernel(qi_tab, ki_tab, q_ref, k_ref, v_ref, lamv_ref, g_ref, o_ref,
                 m1, l1, a1, m2, l2, a2, *, lam_init, tq):
    step = pl.program_id(2)
    qi = qi_tab[step]
    ki = ki_tab[step]

    @pl.when(ki == 0)
    def _():
        for m, l, a in ((m1, l1, a1), (m2, l2, a2)):
            m[...] = jnp.full(m.shape, NEG_INF, F32)
            l[...] = jnp.zeros(l.shape, F32)
            a[...] = jnp.zeros(a.shape, F32)

    def update(diagonal):
        q = q_ref[0] * jnp.asarray(DA_QK_DIM ** -0.5, BF16)
        k = k_ref[0]
        v = v_ref[0]
        lane = lax.broadcasted_iota(I32, q.shape, 1)
        zero = jnp.zeros_like(q)
        halves = (jnp.where(lane < DA_QK_DIM, q, zero), jnp.where(lane >= DA_QK_DIM, q, zero))
        if diagonal:
            row = lax.broadcasted_iota(I32, (tq, tq), 0)
            col = lax.broadcasted_iota(I32, (tq, tq), 1)
            keep = row >= col
        for qh, m, l, a in ((halves[0], m1, l1, a1), (halves[1], m2, l2, a2)):
            s = lax.dot_general(qh, k, (((1,), (1,)), ((), ())), preferred_element_type=F32)
            if diagonal:
                s = jnp.where(keep, s, NEG_INF)
            m_old = m[...]
            m_new = jnp.maximum(m_old, jnp.max(s, axis=-1, keepdims=True))
            p = jnp.exp(s - m_new)
            alpha = jnp.exp(m_old - m_new)
            l[...] = alpha * l[...] + jnp.sum(p, axis=-1, keepdims=True)
            a[...] = alpha * a[...] + jnp.dot(p.astype(BF16), v, preferred_element_type=F32)
            m[...] = m_new

    @pl.when(ki < qi)
    def _():
        update(False)

    @pl.when(ki == qi)
    def _():
        update(True)
        lamv = lamv_ref[...]
        lam = (jnp.exp(jnp.sum(lamv[0:1] * lamv[1:2], axis=-1, keepdims=True))
               - jnp.exp(jnp.sum(lamv[2:3] * lamv[3:4], axis=-1, keepdims=True)) + lam_init)
        o = a1[...] / l1[...] - lam * (a2[...] / l2[...])
        ms = jnp.mean(o * o, axis=-1, keepdims=True)
        o = o * lax.rsqrt(ms + RMS_EPS) * g_ref[...] * (1.0 - lam_init)
        o_ref[0] = o.astype(o_ref.dtype)


def _diff_attention(qkv, lamv, norm_g, lam_init, tq):
    b, s, _ = qkv.shape
    nq = s // tq
    pairs = [(qi, ki) for qi in range(nq) for ki in range(qi + 1)]
    qi_tab = jnp.asarray([p[0] for p in pairs], I32)
    ki_tab = jnp.asarray([p[1] for p in pairs], I32)
    kern = functools.partial(_attn_kernel, lam_init=lam_init, tq=tq)
    grid_spec = pltpu.PrefetchScalarGridSpec(
        num_scalar_prefetch=2,
        grid=(b, DA_HEADS, len(pairs)),
        in_specs=[
            pl.BlockSpec((1, tq, LANES), lambda bi, h, t, qt, kt: (bi, qt[t], h)),
            pl.BlockSpec((1, tq, LANES), lambda bi, h, t, qt, kt: (bi, kt[t], DA_HEADS + h)),
            pl.BlockSpec((1, tq, LANES), lambda bi, h, t, qt, kt: (bi, kt[t], 2 * DA_HEADS + h)),
            pl.BlockSpec((8, LANES), lambda bi, h, t, qt, kt: (0, 0)),
            pl.BlockSpec((1, LANES), lambda bi, h, t, qt, kt: (0, 0)),
        ],
        out_specs=pl.BlockSpec((1, tq, LANES), lambda bi, h, t, qt, kt: (bi, qt[t], h)),
        scratch_shapes=[pltpu.VMEM((tq, 1), F32), pltpu.VMEM((tq, 1), F32), pltpu.VMEM((tq, LANES), F32),
                        pltpu.VMEM((tq, 1), F32), pltpu.VMEM((tq, 1), F32), pltpu.VMEM((tq, LANES), F32)],
    )
    return pl.pallas_call(
        kern,
        grid_spec=grid_spec,
        out_shape=jax.ShapeDtypeStruct((b, s, SEC), BF16),
        compiler_params=_cparams(("parallel", "parallel", "arbitrary")),
        name="diff_attn",
    )(qi_tab, ki_tab, qkv, qkv, qkv, lamv, norm_g)


def _log_sigmoid(x):
    return jnp.minimum(x, 0.0) - jnp.log1p(jnp.exp(-jnp.abs(x)))


def _sigmoid(x):
    return 1.0 / (1.0 + jnp.exp(-x))


def _mlstm_kernel(ml_ref, gt_ref, cw_ref, cb_ref, gb_ref, o_ref, xbuf, cst, mst, *, chunk):
    L = chunk
    HALO = 8
    c = pl.program_id(1)

    @pl.when(c == 0)
    def _():
        xbuf[0:HALO, :] = jnp.zeros((HALO, SEC), F32)
        cst[...] = jnp.zeros(cst.shape, F32)
        mst[...] = jnp.full(mst.shape, NEG_INF, F32)

    xbuf[HALO:HALO + L, :] = ml_ref[0, :, 0:SEC]
    y = cb_ref[...] + cw_ref[CONV_WIDTH - 1:CONV_WIDTH, :] * xbuf[HALO:HALO + L, :]
    for j in range(CONV_WIDTH - 1):
        off = HALO - (CONV_WIDTH - 1) + j
        y = y + cw_ref[j:j + 1, :] * xbuf[off:off + L, :]
    xbuf[0:HALO, :] = xbuf[L:L + HALO, :]
    qk = y * _sigmoid(y)

    gi = gt_ref[0, :, 0:LANES] + gb_ref[:, 0:LANES]
    lf = _log_sigmoid(gt_ref[0, :, LANES:GATE_W] + gb_ref[:, LANES:GATE_W])
    row = lax.broadcasted_iota(I32, (L, L), 0)
    col = lax.broadcasted_iota(I32, (L, L), 1)
    causal = row >= col
    bcum = jnp.dot(causal.astype(F32), lf, preferred_element_type=F32,
                   precision=lax.Precision.HIGHEST)
    g = bcum[L - 1:L, :]
    w_end = g - bcum + gi
    m_loc = jnp.max(w_end, axis=0, keepdims=True)
    e_end = jnp.exp(w_end - m_loc)
    m_prev = mst[...]
    m_new = jnp.maximum(g + m_prev, m_loc)
    a_dec = jnp.exp(g + m_prev - m_new)
    b_dec = jnp.exp(m_loc - m_new)
    mst[...] = m_new
    inter_log = bcum + m_prev
    bcum_t = bcum.T
    gi_t = gi.T

    lane = lax.broadcasted_iota(I32, (L, LANES), 1)
    sub = lax.broadcasted_iota(I32, (LANES, 1), 0)
    ones_col = jnp.where(lane == 0, 1.0, 0.0).astype(BF16)
    for p in range(ML_HEADS // 2):
        q_pair = qk[:, p * LANES:(p + 1) * LANES] * (ML_QK_DIM ** -0.5)
        k_pair = qk[:, 2 * LANES + p * LANES:2 * LANES + (p + 1) * LANES]
        kb = k_pair.astype(BF16)
        c_prev = cst[p].astype(BF16)
        upd = jnp.zeros((LANES, 2 * LANES), F32)
        for hh in range(2):
            h = 2 * p + hh
            head_lanes = (lane >= hh * ML_QK_DIM) & (lane < (hh + 1) * ML_QK_DIM)
            qm = jnp.where(head_lanes, q_pair, 0.0).astype(BF16)
            s = lax.dot_general(qm, kb, (((1,), (1,)), ((), ())), preferred_element_type=F32)
            d = jnp.where(causal, bcum[:, h:h + 1] - bcum_t[h:h + 1, :] + gi_t[h:h + 1, :], NEG_INF)
            il = inter_log[:, h:h + 1]
            m_t = jnp.maximum(il, jnp.max(d, axis=-1, keepdims=True))
            sd = (s * jnp.exp(d - m_t)).astype(BF16)
            inter_w = jnp.exp(il - m_t)
            v_aug = jnp.concatenate(
                [ml_ref[0, :, SEC + h * LANES:SEC + (h + 1) * LANES].astype(BF16), ones_col], axis=1)
            intra = jnp.dot(sd, v_aug, preferred_element_type=F32)
            inter = jnp.dot(qm, c_prev, preferred_element_type=F32)
            num = inter_w * inter[:, 0:LANES] + intra[:, 0:LANES]
            den = inter_w * inter[:, LANES:LANES + 1] + intra[:, LANES:LANES + 1]
            hid = num / jnp.maximum(jnp.abs(den), jnp.exp(-m_t))
            o_gate = ml_ref[0, :, 2 * SEC + h * LANES:2 * SEC + (h + 1) * LANES]
            o_ref[0, :, h * LANES:(h + 1) * LANES] = (hid * _sigmoid(o_gate)).astype(o_ref.dtype)
            ek = jnp.where(head_lanes, e_end[:, h:h + 1] * k_pair, 0.0).astype(BF16)
            upd = upd + lax.dot_general(ek, v_aug, (((0,), (0,)), ((), ())),
                                        preferred_element_type=F32)
        first = sub < ML_QK_DIM
        a_rows = jnp.where(first, a_dec[:, 2 * p:2 * p + 1], a_dec[:, 2 * p + 1:2 * p + 2])
        b_rows = jnp.where(first, b_dec[:, 2 * p:2 * p + 1], b_dec[:, 2 * p + 1:2 * p + 2])
        cst[p] = a_rows * cst[p] + b_rows * upd


def _mlstm(ml, gates, conv_w, conv_b, gate_b, chunk):
    b, s, _ = ml.shape
    kern = functools.partial(_mlstm_kernel, chunk=chunk)
    return pl.pallas_call(
        kern,
        grid=(b, s // chunk),
        in_specs=[pl.BlockSpec((1, chunk, 3 * SEC), lambda bi, c: (bi, c, 0)),
                  pl.BlockSpec((1, chunk, GATE_W), lambda bi, c: (bi, c, 0)),
                  pl.BlockSpec((CONV_WIDTH, SEC), lambda bi, c: (0, 0)),
                  pl.BlockSpec((1, SEC), lambda bi, c: (0, 0)),
                  pl.BlockSpec((1, GATE_W), lambda bi, c: (0, 0))],
        out_specs=pl.BlockSpec((1, chunk, SEC), lambda bi, c: (bi, c, 0)),
        out_shape=jax.ShapeDtypeStruct((b, s, SEC), BF16),
        scratch_shapes=[pltpu.VMEM((chunk + 8, SEC), F32),
                        pltpu.VMEM((ML_HEADS // 2, LANES, 2 * LANES), F32),
                        pltpu.VMEM((1, LANES), F32)],
        compiler_params=_cparams(("parallel", "arbitrary")),
        name="mlstm",
    )(ml, gates, conv_w, conv_b, gate_b)


def _layer_norm(h, g, b):
    mu = jnp.mean(h, axis=-1, keepdims=True)
    hc = h - mu
    var = jnp.mean(hc * hc, axis=-1, keepdims=True)
    return hc * lax.rsqrt(var + LN_EPS) * g + b


def _outproj_router_kernel(da_ref, mlo_ref, x_ref, wo_ref, g_ref, b_ref, wr_ref, br_ref,
                           x1_ref, te_ref, tg_ref, rk_ref, cnt_ref, carry, *, tm):
    i = pl.program_id(0)

    @pl.when(i == 0)
    def _():
        carry[...] = jnp.zeros(carry.shape, F32)

    mix = (jnp.dot(da_ref[...], wo_ref[0:SEC, :], preferred_element_type=F32)
           + jnp.dot(mlo_ref[...], wo_ref[SEC:2 * SEC, :], preferred_element_type=F32))
    x1 = _layer_norm(DN_ALPHA * x_ref[...] + mix, g_ref[...], b_ref[...])
    x1_ref[...] = x1

    logits = jnp.dot(x1, wr_ref[...], preferred_element_type=F32,
                     precision=lax.Precision.HIGHEST) + br_ref[...]
    lane = lax.broadcasted_iota(I32, (tm, LANES), 1)
    work = logits
    hot = []
    vals = []
    for _k in range(TOP_K):
        mx = jnp.max(work, axis=-1, keepdims=True)
        idx = jnp.min(jnp.where(work == mx, lane, LANES), axis=-1, keepdims=True)
        sel = lane == idx
        hot.append(sel)
        vals.append(mx)
        work = jnp.where(sel, NEG_INF, work)
    exps = [jnp.exp(v - vals[0]) for v in vals]
    inv = 1.0 / (exps[0] + exps[1] + exps[2] + exps[3])

    onehot = (hot[0] | hot[1] | hot[2] | hot[3])
    oh = jnp.where(onehot, 1.0, 0.0)
    row = lax.broadcasted_iota(I32, (tm, tm), 0)
    col = lax.broadcasted_iota(I32, (tm, tm), 1)
    before = jnp.dot((row > col).astype(BF16), oh.astype(BF16), preferred_element_type=F32)
    pos = before + carry[...]
    te = jnp.zeros((tm, LANES), I32)
    tg = jnp.zeros((tm, LANES), F32)
    rk = jnp.zeros((tm, LANES), I32)
    for k in range(TOP_K):
        slot = lane == k
        e_k = jnp.min(jnp.where(hot[k], lane, LANES), axis=-1, keepdims=True)
        r_k = jnp.sum(jnp.where(hot[k], pos, 0.0), axis=-1, keepdims=True)
        te = jnp.where(slot, e_k, te)
        tg = jnp.where(slot, exps[k] * inv, tg)
        rk = jnp.where(slot, r_k.astype(I32), rk)
    te_ref[...] = te
    tg_ref[...] = tg
    rk_ref[...] = rk
    carry[...] = carry[...] + jnp.sum(oh, axis=0, keepdims=True)
    cnt_ref[...] = jnp.broadcast_to(carry[...], cnt_ref.shape).astype(I32)


def _outproj_router(da, mlo, x, wo, ln_g, ln_b, wr, br, tm):
    n = x.shape[0]
    kern = functools.partial(_outproj_router_kernel, tm=tm)
    row_blk = lambda w: pl.BlockSpec((tm, w), lambda i: (i, 0))
    const = lambda r, w: pl.BlockSpec((r, w), lambda i: (0, 0))
    return pl.pallas_call(
        kern,
        grid=(n // tm,),
        in_specs=[row_blk(SEC), row_blk(SEC), row_blk(D_MODEL), const(2 * SEC, D_MODEL),
                  const(1, D_MODEL), const(1, D_MODEL), const(D_MODEL, LANES), const(1, LANES)],
        out_specs=[row_blk(D_MODEL), row_blk(LANES), row_blk(LANES), row_blk(LANES), const(8, LANES)],
        out_shape=[jax.ShapeDtypeStruct((n, D_MODEL), F32),
                   jax.ShapeDtypeStruct((n, LANES), I32),
                   jax.ShapeDtypeStruct((n, LANES), F32),
                   jax.ShapeDtypeStruct((n, LANES), I32),
                   jax.ShapeDtypeStruct((8, LANES), I32)],
        scratch_shapes=[pltpu.VMEM((1, LANES), F32)],
        compiler_params=_cparams(("arbitrary",)),
        name="outproj_router",
    )(da, mlo, x, wo, ln_g, ln_b, wr, br)


def _dispatch_kernel(dest_ref, x_ref, xs_in_ref, xs_ref, sem, *, tm):
    del xs_in_ref

    def issue(t, carry):
        for k in range(TOP_K):
            pltpu.make_async_copy(x_ref.at[pl.ds(t, 1), :],
                                  xs_ref.at[pl.ds(dest_ref[t * TOP_K + k], 1), :], sem).start()
        return carry

    lax.fori_loop(0, tm, issue, 0)
    for k in range(TOP_K):
        pltpu.make_async_copy(x_ref, xs_ref.at[pl.ds(0, tm), :], sem).wait()


def _dispatch(x1, dest_flat, xs_zero, tm):
    n = x1.shape[0]
    kern = functools.partial(_dispatch_kernel, tm=tm)
    return pl.pallas_call(
        kern,
        grid=(n // tm,),
        in_specs=[pl.BlockSpec((tm * TOP_K,), lambda i: (i,), memory_space=pltpu.SMEM),
                  pl.BlockSpec((tm, D_MODEL), lambda i: (i, 0)),
                  pl.BlockSpec(memory_space=pl.ANY)],
        out_specs=pl.BlockSpec(memory_space=pl.ANY),
        out_shape=jax.ShapeDtypeStruct(xs_zero.shape, xs_zero.dtype),
        scratch_shapes=[pltpu.SemaphoreType.DMA(())],
        input_output_aliases={2: 0},
        compiler_params=_cparams(("arbitrary",)),
        name="moe_dispatch",
    )(dest_flat, x1, xs_zero)


def _expert_kernel(be_ref, nused_ref, xs_ref, wgu_ref, bgu_ref, wd_ref, bd_ref, ys_ref):
    i = pl.program_id(0)

    @pl.when(i < nused_ref[0])
    def _():
        xb = xs_ref[...].astype(BF16)
        hgu = jnp.dot(xb, wgu_ref[0], preferred_element_type=F32) + bgu_ref[0]
        gate = jnp.minimum(hgu[:, :D_FF], SWIGLU_LIMIT)
        up = jnp.clip(hgu[:, D_FF:], -SWIGLU_LIMIT, SWIGLU_LIMIT)
        glu = gate * _sigmoid(SWIGLU_ALPHA * gate)
        act = ((up + 1.0) * glu).astype(BF16)
        ys_ref[...] = jnp.dot(act, wd_ref[0], preferred_element_type=F32) + bd_ref[0]

    @pl.when(i >= nused_ref[0])
    def _():
        ys_ref[...] = jnp.zeros(ys_ref.shape, F32)


def _experts(xs, block_e, nused, wgu, bgu, wd, bd, tm):
    n_rows = xs.shape[0]
    nb = n_rows // tm

    def rows(i, be, nu):
        return (jnp.minimum(i, nu[0] - 1), 0)

    def per_e(i, be, nu):
        return (be[i], 0, 0)

    grid_spec = pltpu.PrefetchScalarGridSpec(
        num_scalar_prefetch=2,
        grid=(nb,),
        in_specs=[pl.BlockSpec((tm, D_MODEL), rows),
                  pl.BlockSpec((1, D_MODEL, 2 * D_FF), per_e),
                  pl.BlockSpec((1, 1, 2 * D_FF), per_e),
                  pl.BlockSpec((1, D_FF, D_MODEL), per_e),
                  pl.BlockSpec((1, 1, D_MODEL), per_e)],
        out_specs=pl.BlockSpec((tm, D_MODEL), lambda i, be, nu: (i, 0)),
    )
    return pl.pallas_call(
        _expert_kernel,
        grid_spec=grid_spec,
        out_shape=jax.ShapeDtypeStruct((n_rows, D_MODEL), F32),
        compiler_params=_cparams(("arbitrary",)),
        name="moe_experts",
    )(block_e, nused, xs, wgu, bgu, wd, bd)


def _combine_kernel(dest_ref, ys_ref, x1_ref, tg_ref, g_ref, b_ref, o_ref, buf, sem, *, tm):
    def issue(t, carry):
        for k in range(TOP_K):
            pltpu.make_async_copy(ys_ref.at[pl.ds(dest_ref[t * TOP_K + k], 1), :],
                                  buf.at[k, pl.ds(t, 1), :], sem).start()
        return carry

    lax.fori_loop(0, tm, issue, 0)
    for k in range(TOP_K):
        pltpu.make_async_copy(ys_ref.at[pl.ds(0, tm), :], buf.at[k], sem).wait()
    tg = tg_ref[...]
    y = tg[:, 0:1] * buf[0]
    for k in range(1, TOP_K):
        y = y + tg[:, k:k + 1] * buf[k]
    o_ref[...] = _layer_norm(DN_ALPHA * x1_ref[...] + y, g_ref[...], b_ref[...])


def _combine(ys, dest_flat, x1, tg, ln_g, ln_b, tm):
    n = x1.shape[0]
    kern = functools.partial(_combine_kernel, tm=tm)
    return pl.pallas_call(
        kern,
        grid=(n // tm,),
        in_specs=[pl.BlockSpec((tm * TOP_K,), lambda i: (i,), memory_space=pltpu.SMEM),
                  pl.BlockSpec(memory_space=pl.ANY),
                  pl.BlockSpec((tm, D_MODEL), lambda i: (i, 0)),
                  pl.BlockSpec((tm, LANES), lambda i: (i, 0)),
                  pl.BlockSpec((1, D_MODEL), lambda i: (0, 0)),
                  pl.BlockSpec((1, D_MODEL), lambda i: (0, 0))],
        out_specs=pl.BlockSpec((tm, D_MODEL), lambda i: (i, 0)),
        out_shape=jax.ShapeDtypeStruct((n, D_MODEL), F32),
        scratch_shapes=[pltpu.VMEM((TOP_K, tm, D_MODEL), F32), pltpu.SemaphoreType.DMA(())],
        compiler_params=_cparams(("arbitrary",)),
        name="moe_combine",
    )(dest_flat, ys, x1, tg, ln_g, ln_b)


def _pad_lanes(v, width=LANES, value=0.0):
    return jnp.pad(v, ((0, 0), (0, width - v.shape[-1])), constant_values=value)


def _prep_w_in(w):
    body = w[:, :6 * SEC]
    gi = _pad_lanes(w[:, 6 * SEC:6 * SEC + ML_HEADS])
    gf = _pad_lanes(w[:, 6 * SEC + ML_HEADS:6 * SEC + 2 * ML_HEADS])
    return jnp.concatenate([body, gi, gf], axis=1).astype(BF16)


def _layer(x, p, lam_init, cfg):
    n = x.shape[0]
    b, s = cfg["batch"], cfg["seq"]
    tm_e = cfg["tm_expert"]

    qkv, ml, gates = _inproj(x, _prep_w_in(p["w_in"]), cfg["tm_proj"])

    lamv = jnp.zeros((8, LANES), F32)
    for r, nm in enumerate(("lam_q1", "lam_k1", "lam_q2", "lam_k2")):
        lamv = lamv.at[r, :DA_QK_DIM].set(p[nm])
    da = _diff_attention(qkv.reshape(b, s, 3 * SEC), lamv, p["da_norm_g"].reshape(1, LANES),
                         lam_init, cfg["tq"])

    gate_b = jnp.concatenate([_pad_lanes(p["gate_b"][None, :ML_HEADS]),
                              _pad_lanes(p["gate_b"][None, ML_HEADS:])], axis=1)
    mlo = _mlstm(ml.reshape(b, s, 3 * SEC), gates.reshape(b, s, GATE_W), p["conv_w"],
                 p["conv_b"].reshape(1, SEC), gate_b, cfg["chunk"])

    wr = _pad_lanes(p["w_router"])
    br = _pad_lanes(p["b_router"][None, :], value=NEG_INF)
    x1, te, tg, rk, cnt = _outproj_router(
        da.reshape(n, SEC), mlo.reshape(n, SEC), x, p["w_out"].astype(BF16),
        p["ln1_g"].reshape(1, D_MODEL), p["ln1_b"].reshape(1, D_MODEL), wr, br, cfg["tm_proj"])

    counts = cnt[0, :N_EXPERTS]
    padded = (counts + tm_e - 1) // tm_e * tm_e
    pad_ends = jnp.cumsum(padded)
    pad_starts = pad_ends - padded
    top_e = te[:, :TOP_K]
    sel = top_e[:, :, None] == jnp.arange(N_EXPERTS, dtype=I32)[None, None, :]
    dest = rk[:, :TOP_K] + jnp.sum(jnp.where(sel, pad_starts[None, None, :], 0), axis=-1)
    dest_flat = dest.reshape(n * TOP_K).astype(I32)
    n_rows = n * TOP_K + N_EXPERTS * tm_e
    nb = n_rows // tm_e
    blk_start = jnp.arange(nb, dtype=I32) * tm_e
    block_e = jnp.minimum(jnp.sum(blk_start[:, None] >= pad_ends[None, :], axis=-1),
                          N_EXPERTS - 1).astype(I32)
    nused = (pad_ends[-1:] // tm_e).astype(I32)

    xs = _dispatch(x1, dest_flat, jnp.zeros((n_rows, D_MODEL), F32), cfg["tm_disp"])
    ys = _experts(xs, block_e, nused, p["w_gu"].astype(BF16), p["b_gu"][:, None, :],
                  p["w_down"].astype(BF16), p["b_down"][:, None, :], tm_e)
    return _combine(ys, dest_flat, x1, tg, p["ln2_g"].reshape(1, D_MODEL),
                    p["ln2_b"].reshape(1, D_MODEL), cfg["tm_comb"])


def _forward(x, params, cfg):
    b, s, d = x.shape
    h = x.reshape(b * s, d)
    depth = params["w_in"].shape[0]
    for l in range(depth):
        lam_init = 0.8 - 0.6 * math.exp(-0.3 * l)
        p = {k: v[l] for k, v in params.items()}
        h = _layer(h, p, lam_init, cfg)
    return h.reshape(b, s, d)


def kernel(x, w_in, conv_w, conv_b, gate_b, lam_q1, lam_k1, lam_q2, lam_k2, da_norm_g, w_out,
           ln1_g, ln1_b, w_router, b_router, w_gu, b_gu, w_down, b_down, ln2_g, ln2_b):
    params = dict(w_in=w_in, conv_w=conv_w, conv_b=conv_b, gate_b=gate_b, lam_q1=lam_q1,
                  lam_k1=lam_k1, lam_q2=lam_q2, lam_k2=lam_k2, da_norm_g=da_norm_g, w_out=w_out,
                  ln1_g=ln1_g, ln1_b=ln1_b, w_router=w_router, b_router=b_router, w_gu=w_gu,
                  b_gu=b_gu, w_down=w_down, b_down=b_down, ln2_g=ln2_g, ln2_b=ln2_b)
    cfg = dict(batch=x.shape[0], seq=x.shape[1], tm_proj=512, tq=512, chunk=256,
               tm_expert=512, tm_disp=256, tm_comb=256)
    return _forward(x, params, cfg)
```

```python
import functools
import math

import jax
import jax.numpy as jnp
from jax import lax
from jax.experimental import pallas as pl
from jax.experimental.pallas import tpu as pltpu

F32 = jnp.float32
BF16 = jnp.bfloat16
I32 = jnp.int32

D_MODEL = 1024
DEPTH = 4
DA_HEADS = 4
DA_QK_DIM = 64
DA_V_DIM = 128
ML_HEADS = 4
ML_QK_DIM = 64
ML_V_DIM = 128
CONV_WIDTH = 4
N_EXPERTS = 32
TOP_K = 4
D_FF = 1024
SWIGLU_LIMIT = 7.0
SWIGLU_ALPHA = 1.702
DN_ALPHA = (2 * DEPTH) ** 0.25
LN_EPS = 1e-5
RMS_EPS = 1e-5

LANES = 128
SEC = 512
GATE_W = 2 * LANES
IN_W_PAD = 6 * SEC + GATE_W
VMEM_LIMIT = 56 * 1024 * 1024
NEG_INF = float("-inf")


def _cparams(sem):
    return pltpu.CompilerParams(dimension_semantics=sem, vmem_limit_bytes=VMEM_LIMIT)


def _inproj_kernel(x_ref, w_ref, qkv_ref, ml_ref, gate_ref):
    xb = x_ref[...].astype(BF16)
    for c in range(3):
        sl = slice(c * SEC, (c + 1) * SEC)
        qkv_ref[:, sl] = jnp.dot(xb, w_ref[:, sl], preferred_element_type=F32).astype(BF16)
    for c in range(3):
        ml_ref[:, c * SEC:(c + 1) * SEC] = jnp.dot(
            xb, w_ref[:, (3 + c) * SEC:(4 + c) * SEC], preferred_element_type=F32)
    gate_ref[...] = jnp.dot(xb, w_ref[:, 6 * SEC:], preferred_element_type=F32)


def _inproj(x, w, tm):
    n = x.shape[0]
    return pl.pallas_call(
        _inproj_kernel,
        grid=(n // tm,),
        in_specs=[pl.BlockSpec((tm, D_MODEL), lambda i: (i, 0)),
                  pl.BlockSpec((D_MODEL, IN_W_PAD), lambda i: (0, 0))],
        out_specs=[pl.BlockSpec((tm, 3 * SEC), lambda i: (i, 0)),
                   pl.BlockSpec((tm, 3 * SEC), lambda i: (i, 0)),
                   pl.BlockSpec((tm, GATE_W), lambda i: (i, 0))],
        out_shape=[jax.ShapeDtypeStruct((n, 3 * SEC), BF16),
                   jax.ShapeDtypeStruct((n, 3 * SEC), F32),
                   jax.ShapeDtypeStruct((n, GATE_W), F32)],
        compiler_params=_cparams(("parallel",)),
        name="in_proj",
    )(x, w)


def _attn_kernel(qi_tab, ki_tab, q_ref, k_ref, v_ref, lamv_ref, g_ref, o_ref, qs, m_s, acc,
                 *, lam_init, tq, tk):
    step = pl.program_id(2)
    qi = qi_tab[step]
    ki = ki_tab[step]
    ng = tq // tk
    reps = tk // LANES

    @pl.when(ki == 0)
    def _():
        q = q_ref[0] * jnp.asarray(DA_QK_DIM ** -0.5, BF16)
        lane = lax.broadcasted_iota(I32, q.shape, 1)
        zero = jnp.zeros_like(q)
        qs[0:tq, :] = jnp.where(lane < DA_QK_DIM, q, zero)
        qs[tq:2 * tq, :] = jnp.where(lane >= DA_QK_DIM, q, zero)
        m_s[...] = jnp.full(m_s.shape, NEG_INF, F32)
        acc[...] = jnp.zeros(acc.shape, F32)

    def update(diag):
        k = k_ref[0]
        v_aug = jnp.concatenate([v_ref[0], jnp.ones((tk, LANES), BF16)], axis=1)
        groups = [(gi, slice(mp * tq + gi * tk, mp * tq + (gi + 1) * tk))
                  for gi in range(ng) if gi >= diag for mp in range(2)]
        ss = [lax.dot_general(qs[g, :], k, (((1,), (1,)), ((), ())), preferred_element_type=F32)
              for _, g in groups]
        if diag >= 0:
            row = lax.broadcasted_iota(I32, (tk, tk), 0)
            col = lax.broadcasted_iota(I32, (tk, tk), 1)
            ss = [jnp.where(row >= col, s, NEG_INF) if gi == diag else s
                  for (gi, _), s in zip(groups, ss)]
        for (_, g), s in zip(groups, ss):
            m_old = m_s[g, :]
            m_new = jnp.maximum(m_old, jnp.max(s, axis=-1, keepdims=True))
            p = jnp.exp(s - jnp.tile(m_new, (1, reps)))
            alpha = jnp.exp(m_old - m_new)
            pv = jnp.dot(p.astype(BF16), v_aug, preferred_element_type=F32)
            acc[g, :] = jnp.tile(alpha, (1, 2)) * acc[g, :] + pv
            m_s[g, :] = m_new

    @pl.when(ki < qi * ng)
    def _():
        update(-1)

    for d in range(ng):
        @pl.when(ki == qi * ng + d)
        def _(d=d):
            update(d)

    @pl.when(ki == qi * ng + ng - 1)
    def _():
        lamv = lamv_ref[...]
        lam = (jnp.exp(jnp.sum(lamv[0:1] * lamv[1:2], axis=-1, keepdims=True))
               - jnp.exp(jnp.sum(lamv[2:3] * lamv[3:4], axis=-1, keepdims=True)) + lam_init)
        on = acc[:, 0:LANES] / acc[:, LANES:2 * LANES]
        o = on[0:tq] - lam * on[tq:2 * tq]
        ms = jnp.mean(o * o, axis=-1, keepdims=True)
        o = o * lax.rsqrt(ms + RMS_EPS) * g_ref[...] * (1.0 - lam_init)
        o_ref[0] = o.astype(o_ref.dtype)


def _diff_attention(qkv, lamv, norm_g, lam_init, tq, tk):
    b, s, _ = qkv.shape
    ng = tq // tk
    pairs = [(qi, ki) for qi in range(s // tq) for ki in range((qi + 1) * ng)]
    qi_tab = jnp.asarray([p[0] for p in pairs], I32)
    ki_tab = jnp.asarray([p[1] for p in pairs], I32)
    kern = functools.partial(_attn_kernel, lam_init=lam_init, tq=tq, tk=tk)
    grid_spec = pltpu.PrefetchScalarGridSpec(
        num_scalar_prefetch=2,
        grid=(b, DA_HEADS, len(pairs)),
        in_specs=[
            pl.BlockSpec((1, tq, LANES), lambda bi, h, t, qt, kt: (bi, qt[t], h)),
            pl.BlockSpec((1, tk, LANES), lambda bi, h, t, qt, kt: (bi, kt[t], DA_HEADS + h)),
            pl.BlockSpec((1, tk, LANES), lambda bi, h, t, qt, kt: (bi, kt[t], 2 * DA_HEADS + h)),
            pl.BlockSpec((8, LANES), lambda bi, h, t, qt, kt: (0, 0)),
            pl.BlockSpec((1, LANES), lambda bi, h, t, qt, kt: (0, 0)),
        ],
        out_specs=pl.BlockSpec((1, tq, LANES), lambda bi, h, t, qt, kt: (bi, qt[t], h)),
        scratch_shapes=[pltpu.VMEM((2 * tq, LANES), BF16), pltpu.VMEM((2 * tq, LANES), F32),
                        pltpu.VMEM((2 * tq, 2 * LANES), F32)],
    )
    return pl.pallas_call(
        kern,
        grid_spec=grid_spec,
        out_shape=jax.ShapeDtypeStruct((b, s, SEC), BF16),
        compiler_params=_cparams(("parallel", "parallel", "arbitrary")),
        name="diff_attn",
    )(qi_tab, ki_tab, qkv, qkv, qkv, lamv, norm_g)


def _log_sigmoid(x):
    return jnp.minimum(x, 0.0) - jnp.log1p(jnp.exp(-jnp.abs(x)))


def _sigmoid(x):
    return 1.0 / (1.0 + jnp.exp(-x))


def _mlstm_kernel(ml_ref, gt_ref, cw_ref, cb_ref, gb_ref, o_ref, xbuf, cst, mst, *, chunk):
    L = chunk
    HALO = 8
    c = pl.program_id(1)

    @pl.when(c == 0)
    def _():
        xbuf[0:HALO, :] = jnp.zeros((HALO, SEC), F32)
        cst[...] = jnp.zeros(cst.shape, F32)
        mst[...] = jnp.full(mst.shape, NEG_INF, F32)

    xbuf[HALO:HALO + L, :] = ml_ref[0, :, 0:SEC]
    y = cb_ref[...] + cw_ref[CONV_WIDTH - 1:CONV_WIDTH, :] * xbuf[HALO:HALO + L, :]
    for j in range(CONV_WIDTH - 1):
        off = HALO - (CONV_WIDTH - 1) + j
        y = y + cw_ref[j:j + 1, :] * xbuf[off:off + L, :]
    xbuf[0:HALO, :] = xbuf[L:L + HALO, :]
    qk = y * _sigmoid(y)

    gi = gt_ref[0, :, 0:LANES] + gb_ref[:, 0:LANES]
    lf = _log_sigmoid(gt_ref[0, :, LANES:GATE_W] + gb_ref[:, LANES:GATE_W])
    row = lax.broadcasted_iota(I32, (L, L), 0)
    col = lax.broadcasted_iota(I32, (L, L), 1)
    causal = row >= col
    bcum = jnp.dot(causal.astype(F32), lf, preferred_element_type=F32,
                   precision=lax.Precision.HIGHEST)
    g = bcum[L - 1:L, :]
    w_end = g - bcum + gi
    m_loc = jnp.max(w_end, axis=0, keepdims=True)
    e_end = jnp.exp(w_end - m_loc)
    m_prev = mst[...]
    m_new = jnp.maximum(g + m_prev, m_loc)
    a_dec = jnp.exp(g + m_prev - m_new)
    b_dec = jnp.exp(m_loc - m_new)
    mst[...] = m_new
    inter_log = bcum + m_prev
    bcum_t = bcum.T
    gi_t = gi.T

    lane = lax.broadcasted_iota(I32, (L, LANES), 1)
    sub = lax.broadcasted_iota(I32, (LANES, 1), 0)
    ones_col = jnp.where(lane == 0, 1.0, 0.0).astype(BF16)
    for p in range(ML_HEADS // 2):
        q_pair = qk[:, p * LANES:(p + 1) * LANES] * (ML_QK_DIM ** -0.5)
        k_pair = qk[:, 2 * LANES + p * LANES:2 * LANES + (p + 1) * LANES]
        kb = k_pair.astype(BF16)
        c_prev = cst[p].astype(BF16)
        upd = jnp.zeros((LANES, 2 * LANES), F32)
        for hh in range(2):
            h = 2 * p + hh
            head_lanes = (lane >= hh * ML_QK_DIM) & (lane < (hh + 1) * ML_QK_DIM)
            qm = jnp.where(head_lanes, q_pair, 0.0).astype(BF16)
            s = lax.dot_general(qm, kb, (((1,), (1,)), ((), ())), preferred_element_type=F32)
            d = jnp.where(causal, bcum[:, h:h + 1] - bcum_t[h:h + 1, :] + gi_t[h:h + 1, :], NEG_INF)
            il = inter_log[:, h:h + 1]
            m_t = jnp.maximum(il, jnp.max(d, axis=-1, keepdims=True))
            sd = (s * jnp.exp(d - m_t)).astype(BF16)
            inter_w = jnp.exp(il - m_t)
            v_aug = jnp.concatenate(
                [ml_ref[0, :, SEC + h * LANES:SEC + (h + 1) * LANES].astype(BF16), ones_col], axis=1)
            intra = jnp.dot(sd, v_aug, preferred_element_type=F32)
            inter = jnp.dot(qm, c_prev, preferred_element_type=F32)
            num = inter_w * inter[:, 0:LANES] + intra[:, 0:LANES]
            den = inter_w * inter[:, LANES:LANES + 1] + intra[:, LANES:LANES + 1]
            hid = num / jnp.maximum(jnp.abs(den), jnp.exp(-m_t))
            o_gate = ml_ref[0, :, 2 * SEC + h * LANES:2 * SEC + (h + 1) * LANES]
            o_ref[0, :, h * LANES:(h + 1) * LANES] = (hid * _sigmoid(o_gate)).astype(o_ref.dtype)
            ek = jnp.where(head_lanes, e_end[:, h:h + 1] * k_pair, 0.0).astype(BF16)
            upd = upd + lax.dot_general(ek, v_aug, (((0,), (0,)), ((), ())),
                                        preferred_element_type=F32)
        first = sub < ML_QK_DIM
        a_rows = jnp.where(first, a_dec[:, 2 * p:2 * p + 1], a_dec[:, 2 * p + 1:2 * p + 2])
        b_rows = jnp.where(first, b_dec[:, 2 * p:2 * p + 1], b_dec[:, 2 * p + 1:2 * p + 2])
        cst[p] = a_rows * cst[p] + b_rows * upd


def _mlstm(ml, gates, conv_w, conv_b, gate_b, chunk):
    b, s, _ = ml.shape
    kern = functools.partial(_mlstm_kernel, chunk=chunk)
    return pl.pallas_call(
        kern,
        grid=(b, s // chunk),
        in_specs=[pl.BlockSpec((1, chunk, 3 * SEC), lambda bi, c: (bi, c, 0)),
                  pl.BlockSpec((1, chunk, GATE_W), lambda bi, c: (bi, c, 0)),
                  pl.BlockSpec((CONV_WIDTH, SEC), lambda bi, c: (0, 0)),
                  pl.BlockSpec((1, SEC), lambda bi, c: (0, 0)),
                  pl.BlockSpec((1, GATE_W), lambda bi, c: (0, 0))],
        out_specs=pl.BlockSpec((1, chunk, SEC), lambda bi, c: (bi, c, 0)),
        out_shape=jax.ShapeDtypeStruct((b, s, SEC), BF16),
        scratch_shapes=[pltpu.VMEM((chunk + 8, SEC), F32),
                        pltpu.VMEM((ML_HEADS // 2, LANES, 2 * LANES), F32),
                        pltpu.VMEM((1, LANES), F32)],
        compiler_params=_cparams(("parallel", "arbitrary")),
        name="mlstm",
    )(ml, gates, conv_w, conv_b, gate_b)


SUBL = D_MODEL // LANES


def _to_tiles(ref, val):
    rows = val.shape[0]
    for s in range(SUBL):
        ref[pl.ds(s, rows, stride=SUBL), :] = val[:, s * LANES:(s + 1) * LANES]


def _tile(ref, t):
    return ref.at[pl.ds(pl.multiple_of(t * SUBL, SUBL), SUBL), :]


def _from_tiles(ref, base, rows):
    return jnp.concatenate(
        [ref[pl.ds(base * SUBL + s, rows, stride=SUBL), :] for s in range(SUBL)], axis=1)


def _layer_norm(h, g, b):
    mu = jnp.mean(h, axis=-1, keepdims=True)
    hc = h - mu
    var = jnp.mean(hc * hc, axis=-1, keepdims=True)
    return hc * lax.rsqrt(var + LN_EPS) * g + b


def _outproj_router_kernel(da_ref, mlo_ref, x_ref, wo_ref, g_ref, b_ref, wr_ref, br_ref,
                           x1_ref, x1t_ref, te_ref, tg_ref, rk_ref, cnt_ref, carry, *, tm):
    i = pl.program_id(0)

    @pl.when(i == 0)
    def _():
        carry[...] = jnp.zeros(carry.shape, F32)

    mix = (jnp.dot(da_ref[...], wo_ref[0:SEC, :], preferred_element_type=F32)
           + jnp.dot(mlo_ref[...], wo_ref[SEC:2 * SEC, :], preferred_element_type=F32))
    x1 = _layer_norm(DN_ALPHA * x_ref[...] + mix, g_ref[...], b_ref[...])
    x1_ref[...] = x1
    _to_tiles(x1t_ref, x1)

    x_hi = x1.astype(BF16)
    x_lo = (x1 - x_hi.astype(F32)).astype(BF16)
    hh_hl = jnp.dot(x_hi, wr_ref[...], preferred_element_type=F32)
    lh = jnp.dot(x_lo, wr_ref[:, 0:LANES], preferred_element_type=F32)
    logits = hh_hl[:, 0:LANES] + (hh_hl[:, LANES:2 * LANES] + lh) + br_ref[...]
    lane = lax.broadcasted_iota(I32, (tm, LANES), 1)
    lane_f = lane.astype(F32)
    work = logits
    hot = []
    vals = []
    idxs = []
    for _k in range(TOP_K):
        mx = jnp.max(work, axis=-1, keepdims=True)
        idx = jnp.min(jnp.where(work == mx, lane_f, float(LANES)), axis=-1, keepdims=True)
        sel = lane_f == idx
        hot.append(sel)
        vals.append(mx)
        idxs.append(idx)
        work = jnp.where(sel, NEG_INF, work)
    exps = [jnp.exp(v - vals[0]) for v in vals]
    inv = 1.0 / (exps[0] + exps[1] + exps[2] + exps[3])

    onehot = (hot[0] | hot[1] | hot[2] | hot[3])
    oh = jnp.where(onehot, 1.0, 0.0)
    row = lax.broadcasted_iota(I32, (tm, tm), 0)
    col = lax.broadcasted_iota(I32, (tm, tm), 1)
    before = jnp.dot((row > col).astype(BF16), oh.astype(BF16), preferred_element_type=F32)
    pos = before + carry[...]
    te = jnp.zeros((tm, LANES), I32)
    tg = jnp.zeros((tm, LANES), F32)
    rk = jnp.zeros((tm, LANES), I32)
    for k in range(TOP_K):
        slot = lane == k
        r_k = jnp.sum(jnp.where(hot[k], pos, 0.0), axis=-1, keepdims=True)
        te = jnp.where(slot, idxs[k].astype(I32), te)
        tg = jnp.where(slot, exps[k] * inv, tg)
        rk = jnp.where(slot, r_k.astype(I32), rk)
    te_ref[...] = te
    tg_ref[...] = tg
    rk_ref[...] = rk
    carry[...] = carry[...] + jnp.sum(oh, axis=0, keepdims=True)
    cnt_ref[...] = jnp.broadcast_to(carry[...], cnt_ref.shape).astype(I32)


def _outproj_router(da, mlo, x, wo, ln_g, ln_b, wr, br, tm):
    n = x.shape[0]
    kern = functools.partial(_outproj_router_kernel, tm=tm)
    row_blk = lambda w: pl.BlockSpec((tm, w), lambda i: (i, 0))
    const = lambda r, w: pl.BlockSpec((r, w), lambda i: (0, 0))
    return pl.pallas_call(
        kern,
        grid=(n // tm,),
        in_specs=[row_blk(SEC), row_blk(SEC), row_blk(D_MODEL), const(2 * SEC, D_MODEL),
                  const(1, D_MODEL), const(1, D_MODEL), const(D_MODEL, 2 * LANES), const(1, LANES)],
        out_specs=[row_blk(D_MODEL), pl.BlockSpec((tm * SUBL, LANES), lambda i: (i, 0)),
                   row_blk(LANES), row_blk(LANES), row_blk(LANES), const(8, LANES)],
        out_shape=[jax.ShapeDtypeStruct((n, D_MODEL), F32),
                   jax.ShapeDtypeStruct((n * SUBL, LANES), F32),
                   jax.ShapeDtypeStruct((n, LANES), I32),
                   jax.ShapeDtypeStruct((n, LANES), F32),
                   jax.ShapeDtypeStruct((n, LANES), I32),
                   jax.ShapeDtypeStruct((8, LANES), I32)],
        scratch_shapes=[pltpu.VMEM((1, LANES), F32)],
        compiler_params=_cparams(("arbitrary",)),
        name="outproj_router",
    )(da, mlo, x, wo, ln_g, ln_b, wr, br)


def _dispatch_kernel(dest_ref, lo_ref, hi_ref, x_ref, xs_ref, zeros, sem, zsem, *, tm):
    i = pl.program_id(0)

    def issue(t, carry):
        for k in range(TOP_K):
            pltpu.make_async_copy(_tile(x_ref, t), _tile(xs_ref, dest_ref[t * TOP_K + k]),
                                  sem).start(priority=k % 2)
        return carry

    lax.fori_loop(0, tm, issue, 0)

    @pl.when(i == 0)
    def _():
        zeros[...] = jnp.zeros(zeros.shape, F32)

        def per_expert(e, carry):
            def fill(r, c):
                pltpu.make_async_copy(zeros, _tile(xs_ref, r), zsem).start()
                return c

            def drain(r, c):
                pltpu.make_async_copy(zeros, _tile(xs_ref, r), zsem).wait()
                return c

            lax.fori_loop(lo_ref[e], hi_ref[e], fill, 0)
            lax.fori_loop(lo_ref[e], hi_ref[e], drain, 0)
            return carry

        lax.fori_loop(0, N_EXPERTS, per_expert, 0)

    for k in range(TOP_K):
        pltpu.make_async_copy(x_ref, xs_ref.at[pl.ds(0, tm * SUBL), :], sem).wait()


def _dispatch(x1t, dest_flat, seg_lo, seg_hi, n_rows, tm):
    n = x1t.shape[0] // SUBL
    kern = functools.partial(_dispatch_kernel, tm=tm)
    return pl.pallas_call(
        kern,
        grid=(n // tm,),
        in_specs=[pl.BlockSpec((tm * TOP_K,), lambda i: (i,), memory_space=pltpu.SMEM),
                  pl.BlockSpec(memory_space=pltpu.SMEM),
                  pl.BlockSpec(memory_space=pltpu.SMEM),
                  pl.BlockSpec((tm * SUBL, LANES), lambda i: (i, 0))],
        out_specs=pl.BlockSpec(memory_space=pl.ANY),
        out_shape=jax.ShapeDtypeStruct((n_rows * SUBL, LANES), F32),
        scratch_shapes=[pltpu.VMEM((SUBL, LANES), F32), pltpu.SemaphoreType.DMA(()),
                        pltpu.SemaphoreType.DMA(())],
        compiler_params=_cparams(("arbitrary",)),
        name="moe_dispatch",
    )(dest_flat, seg_lo, seg_hi, x1t)


def _expert_kernel(be_ref, nused_ref, xs_ref, wgu_ref, bgu_ref, wd_ref, bd_ref, ys_ref,
                   wgu_b, wd_b, *, tm):
    i = pl.program_id(0)
    live = i < nused_ref[0]
    fresh = jnp.logical_or(i == 0, be_ref[i] != be_ref[jnp.maximum(i - 1, 0)])

    @pl.when(jnp.logical_and(live, fresh))
    def _():
        wgu_b[...] = wgu_ref[0].astype(BF16)
        wd_b[...] = wd_ref[0].astype(BF16)

    @pl.when(live)
    def _():
        xb = _from_tiles(xs_ref, 0, tm).astype(BF16)
        hgu = jnp.dot(xb, wgu_b[...], preferred_element_type=F32) + bgu_ref[0]
        gate = jnp.minimum(hgu[:, :D_FF], SWIGLU_LIMIT)
        up = jnp.clip(hgu[:, D_FF:], -SWIGLU_LIMIT, SWIGLU_LIMIT)
        glu = gate * _sigmoid(SWIGLU_ALPHA * gate)
        act = ((up + 1.0) * glu).astype(BF16)
        _to_tiles(ys_ref, jnp.dot(act, wd_b[...], preferred_element_type=F32) + bd_ref[0])

    @pl.when(jnp.logical_not(live))
    def _():
        ys_ref[...] = jnp.zeros(ys_ref.shape, F32)


def _experts(xs, block_e, nused, wgu, bgu, wd, bd, layer, tm):
    n_rows = xs.shape[0] // SUBL
    nb = n_rows // tm

    def rows(i, be, nu):
        return (jnp.minimum(i, nu[0] - 1), 0)

    def per_e(i, be, nu):
        return (layer * N_EXPERTS + be[i], 0, 0)

    grid_spec = pltpu.PrefetchScalarGridSpec(
        num_scalar_prefetch=2,
        grid=(nb,),
        in_specs=[pl.BlockSpec((tm * SUBL, LANES), rows),
                  pl.BlockSpec((1, D_MODEL, 2 * D_FF), per_e),
                  pl.BlockSpec((1, 1, 2 * D_FF), per_e),
                  pl.BlockSpec((1, D_FF, D_MODEL), per_e),
                  pl.BlockSpec((1, 1, D_MODEL), per_e)],
        out_specs=pl.BlockSpec((tm * SUBL, LANES), lambda i, be, nu: (i, 0)),
        scratch_shapes=[pltpu.VMEM((D_MODEL, 2 * D_FF), BF16), pltpu.VMEM((D_FF, D_MODEL), BF16)],
    )
    return pl.pallas_call(
        functools.partial(_expert_kernel, tm=tm),
        grid_spec=grid_spec,
        out_shape=jax.ShapeDtypeStruct((n_rows * SUBL, LANES), F32),
        compiler_params=_cparams(("arbitrary",)),
        name="moe_experts",
    )(block_e, nused, xs, wgu, bgu, wd, bd)


def _combine_kernel(dest_ref, ys_ref, x1_ref, tg_ref, g_ref, b_ref, o_ref, buf, sem, *, tm):
    def issue(t, carry):
        for k in range(TOP_K):
            pltpu.make_async_copy(_tile(ys_ref, dest_ref[t * TOP_K + k]), _tile(buf, k * tm + t),
                                  sem).start(priority=k % 2)
        return carry

    lax.fori_loop(0, tm, issue, 0)
    for k in range(TOP_K):
        pltpu.make_async_copy(ys_ref.at[pl.ds(0, tm * SUBL), :],
                              buf.at[pl.ds(k * tm * SUBL, tm * SUBL), :], sem).wait()
    tg = tg_ref[...]
    y = tg[:, 0:1] * _from_tiles(buf, 0, tm)
    for k in range(1, TOP_K):
        y = y + tg[:, k:k + 1] * _from_tiles(buf, k * tm, tm)
    o_ref[...] = _layer_norm(DN_ALPHA * x1_ref[...] + y, g_ref[...], b_ref[...])


def _combine(ys, dest_flat, x1, tg, ln_g, ln_b, tm):
    n = x1.shape[0]
    kern = functools.partial(_combine_kernel, tm=tm)
    return pl.pallas_call(
        kern,
        grid=(n // tm,),
        in_specs=[pl.BlockSpec((tm * TOP_K,), lambda i: (i,), memory_space=pltpu.SMEM),
                  pl.BlockSpec(memory_space=pl.ANY),
                  pl.BlockSpec((tm, D_MODEL), lambda i: (i, 0)),
                  pl.BlockSpec((tm, LANES), lambda i: (i, 0)),
                  pl.BlockSpec((1, D_MODEL), lambda i: (0, 0)),
                  pl.BlockSpec((1, D_MODEL), lambda i: (0, 0))],
        out_specs=pl.BlockSpec((tm, D_MODEL), lambda i: (i, 0)),
        out_shape=jax.ShapeDtypeStruct((n, D_MODEL), F32),
        scratch_shapes=[pltpu.VMEM((TOP_K * tm * SUBL, LANES), F32), pltpu.SemaphoreType.DMA(())],
        compiler_params=_cparams(("arbitrary",)),
        name="moe_combine",
    )(dest_flat, ys, x1, tg, ln_g, ln_b)


def _pad_lanes(v, width=LANES, value=0.0):
    return jnp.pad(v, ((0, 0), (0, width - v.shape[-1])), constant_values=value)


def _prep_w_in(w):
    body = w[:, :6 * SEC]
    gi = _pad_lanes(w[:, 6 * SEC:6 * SEC + ML_HEADS])
    gf = _pad_lanes(w[:, 6 * SEC + ML_HEADS:6 * SEC + 2 * ML_HEADS])
    return jnp.concatenate([body, gi, gf], axis=1).astype(BF16)


def _layer(x, p, big, layer, lam_init, cfg):
    n = x.shape[0]
    b, s = cfg["batch"], cfg["seq"]
    tm_e = cfg["tm_expert"]

    qkv, ml, gates = _inproj(x, _prep_w_in(p["w_in"]), cfg["tm_proj"])

    lamv = jnp.zeros((8, LANES), F32)
    for r, nm in enumerate(("lam_q1", "lam_k1", "lam_q2", "lam_k2")):
        lamv = lamv.at[r, :DA_QK_DIM].set(p[nm])
    da = _diff_attention(qkv.reshape(b, s, 3 * SEC), lamv, p["da_norm_g"].reshape(1, LANES),
                         lam_init, cfg["tq"], cfg["tk"])

    gate_b = jnp.concatenate([_pad_lanes(p["gate_b"][None, :ML_HEADS]),
                              _pad_lanes(p["gate_b"][None, ML_HEADS:])], axis=1)
    mlo = _mlstm(ml.reshape(b, s, 3 * SEC), gates.reshape(b, s, GATE_W), p["conv_w"],
                 p["conv_b"].reshape(1, SEC), gate_b, cfg["chunk"])

    wr = _pad_lanes(p["w_router"])
    wr_hi = wr.astype(BF16)
    wr = jnp.concatenate([wr_hi, (wr - wr_hi.astype(F32)).astype(BF16)], axis=1)
    br = _pad_lanes(p["b_router"][None, :], value=NEG_INF)
    x1, x1t, te, tg, rk, cnt = _outproj_router(
        da.reshape(n, SEC), mlo.reshape(n, SEC), x, p["w_out"].astype(BF16),
        p["ln1_g"].reshape(1, D_MODEL), p["ln1_b"].reshape(1, D_MODEL), wr, br, cfg["tm_proj"])

    counts = cnt[0, :N_EXPERTS]
    padded = (counts + tm_e - 1) // tm_e * tm_e
    pad_ends = jnp.cumsum(padded)
    pad_starts = pad_ends - padded
    top_e = te[:, :TOP_K]
    sel = top_e[:, :, None] == jnp.arange(N_EXPERTS, dtype=I32)[None, None, :]
    dest = rk[:, :TOP_K] + jnp.sum(jnp.where(sel, pad_starts[None, None, :], 0), axis=-1)
    dest_flat = dest.reshape(n * TOP_K).astype(I32)
    n_rows = n * TOP_K + N_EXPERTS * tm_e
    nb = n_rows // tm_e
    blk_start = jnp.arange(nb, dtype=I32) * tm_e
    block_e = jnp.minimum(jnp.sum(blk_start[:, None] >= pad_ends[None, :], axis=-1),
                          N_EXPERTS - 1).astype(I32)
    nused = (pad_ends[-1:] // tm_e).astype(I32)
    seg_lo = (pad_starts + counts).astype(I32)
    seg_hi = pad_ends.at[N_EXPERTS - 1].set(n_rows).astype(I32)

    xs = _dispatch(x1t, dest_flat, seg_lo, seg_hi, n_rows, cfg["tm_disp"])
    ys = _experts(xs, block_e, nused, big["w_gu"], big["b_gu"], big["w_down"], big["b_down"],
                  layer, tm_e)
    return _combine(ys, dest_flat, x1, tg, p["ln2_g"].reshape(1, D_MODEL),
                    p["ln2_b"].reshape(1, D_MODEL), cfg["tm_comb"])


def _forward(x, params, cfg):
    b, s, d = x.shape
    h = x.reshape(b * s, d)
    depth = params["w_in"].shape[0]
    big = dict(w_gu=params["w_gu"].reshape(depth * N_EXPERTS, D_MODEL, 2 * D_FF),
               b_gu=params["b_gu"].reshape(depth * N_EXPERTS, 1, 2 * D_FF),
               w_down=params["w_down"].reshape(depth * N_EXPERTS, D_FF, D_MODEL),
               b_down=params["b_down"].reshape(depth * N_EXPERTS, 1, D_MODEL))
    for l in range(depth):
        lam_init = 0.8 - 0.6 * math.exp(-0.3 * l)
        p = {k: v[l] for k, v in params.items() if k not in big}
        h = _layer(h, p, big, l, lam_init, cfg)
    return h.reshape(b, s, d)


def kernel(x, w_in, conv_w, conv_b, gate_b, lam_q1, lam_k1, lam_q2, lam_k2, da_norm_g, w_out,
           ln1_g, ln1_b, w_router, b_router, w_gu, b_gu, w_down, b_down, ln2_g, ln2_b):
    params = dict(w_in=w_in, conv_w=conv_w, conv_b=conv_b, gate_b=gate_b, lam_q1=lam_q1,
                  lam_k1=lam_k1, lam_q2=lam_q2, lam_k2=lam_k2, da_norm_g=da_norm_g, w_out=w_out,
                  ln1_g=ln1_g, ln1_b=ln1_b, w_router=w_router, b_router=b_router, w_gu=w_gu,
                  b_gu=b_gu, w_down=w_down, b_down=b_down, ln2_g=ln2_g, ln2_b=ln2_b)
    cfg = dict(batch=x.shape[0], seq=x.shape[1], tm_proj=512, tq=2048, tk=512, chunk=256,
               tm_expert=512, tm_disp=256, tm_comb=256)
    return _forward(x, params, cfg)
```

```python
import functools
import math

import jax
import jax.numpy as jnp
from jax import lax
from jax.experimental import pallas as pl
from jax.experimental.pallas import tpu as pltpu

F32 = jnp.float32
BF16 = jnp.bfloat16
I32 = jnp.int32

D_MODEL = 1024
DEPTH = 4
DA_HEADS = 4
DA_QK_DIM = 64
DA_V_DIM = 128
ML_HEADS = 4
ML_QK_DIM = 64
ML_V_DIM = 128
CONV_WIDTH = 4
N_EXPERTS = 32
TOP_K = 4
D_FF = 1024
SWIGLU_LIMIT = 7.0
SWIGLU_ALPHA = 1.702
DN_ALPHA = (2 * DEPTH) ** 0.25
LN_EPS = 1e-5
RMS_EPS = 1e-5

LANES = 128
SEC = 512
GATE_W = 2 * LANES
IN_W_PAD = 6 * SEC + GATE_W
VMEM_LIMIT = 56 * 1024 * 1024
NEG_INF = float("-inf")


def _cparams(sem):
    return pltpu.CompilerParams(dimension_semantics=sem, vmem_limit_bytes=VMEM_LIMIT)


def _inproj_kernel(x_ref, w_ref, qkv_ref, ml_ref, gate_ref):
    xb = x_ref[...].astype(BF16)
    for c in range(3):
        sl = slice(c * SEC, (c + 1) * SEC)
        qkv_ref[:, sl] = jnp.dot(xb, w_ref[:, sl], preferred_element_type=F32).astype(BF16)
    for c in range(3):
        ml_ref[:, c * SEC:(c + 1) * SEC] = jnp.dot(
            xb, w_ref[:, (3 + c) * SEC:(4 + c) * SEC], preferred_element_type=F32)
    gate_ref[...] = jnp.dot(xb, w_ref[:, 6 * SEC:], preferred_element_type=F32)


def _inproj(x, w, tm):
    n = x.shape[0]
    return pl.pallas_call(
        _inproj_kernel,
        grid=(n // tm,),
        in_specs=[pl.BlockSpec((tm, D_MODEL), lambda i: (i, 0)),
                  pl.BlockSpec((D_MODEL, IN_W_PAD), lambda i: (0, 0))],
        out_specs=[pl.BlockSpec((tm, 3 * SEC), lambda i: (i, 0)),
                   pl.BlockSpec((tm, 3 * SEC), lambda i: (i, 0)),
                   pl.BlockSpec((tm, GATE_W), lambda i: (i, 0))],
        out_shape=[jax.ShapeDtypeStruct((n, 3 * SEC), BF16),
                   jax.ShapeDtypeStruct((n, 3 * SEC), F32),
                   jax.ShapeDtypeStruct((n, GATE_W), F32)],
        compiler_params=_cparams(("parallel",)),
        name="in_proj",
    )(x, w)


def _attn_kernel(qi_tab, ki_tab, q_ref, k_ref, v_ref, lamv_ref, g_ref, o_ref, qs, m_s, acc,
                 *, lam_init, tq, tk):
    step = pl.program_id(2)
    qi = qi_tab[step]
    ki = ki_tab[step]
    ng = tq // tk
    reps = tk // LANES

    @pl.when(ki == 0)
    def _():
        q = q_ref[0] * jnp.asarray(DA_QK_DIM ** -0.5, BF16)
        lane = lax.broadcasted_iota(I32, q.shape, 1)
        zero = jnp.zeros_like(q)
        qs[0:tq, :] = jnp.where(lane < DA_QK_DIM, q, zero)
        qs[tq:2 * tq, :] = jnp.where(lane >= DA_QK_DIM, q, zero)
        m_s[...] = jnp.full(m_s.shape, NEG_INF, F32)
        acc[...] = jnp.zeros(acc.shape, F32)

    def update(diag):
        k = k_ref[0]
        v_aug = jnp.concatenate([v_ref[0], jnp.ones((tk, LANES), BF16)], axis=1)
        groups = [(gi, slice(mp * tq + gi * tk, mp * tq + (gi + 1) * tk))
                  for gi in range(ng) if gi >= diag for mp in range(2)]
        ss = [lax.dot_general(qs[g, :], k, (((1,), (1,)), ((), ())), preferred_element_type=F32)
              for _, g in groups]
        if diag >= 0:
            row = lax.broadcasted_iota(I32, (tk, tk), 0)
            col = lax.broadcasted_iota(I32, (tk, tk), 1)
            ss = [jnp.where(row >= col, s, NEG_INF) if gi == diag else s
                  for (gi, _), s in zip(groups, ss)]
        for (_, g), s in zip(groups, ss):
            m_old = m_s[g, :]
            m_new = jnp.maximum(m_old, jnp.max(s, axis=-1, keepdims=True))
            p = jnp.exp(s - jnp.tile(m_new, (1, reps)))
            alpha = jnp.exp(m_old - m_new)
            pv = jnp.dot(p.astype(BF16), v_aug, preferred_element_type=F32)
            acc[g, :] = jnp.tile(alpha, (1, 2)) * acc[g, :] + pv
            m_s[g, :] = m_new

    @pl.when(ki < qi * ng)
    def _():
        update(-1)

    for d in range(ng):
        @pl.when(ki == qi * ng + d)
        def _(d=d):
            update(d)

    @pl.when(ki == qi * ng + ng - 1)
    def _():
        lamv = lamv_ref[...]
        lam = (jnp.exp(jnp.sum(lamv[0:1] * lamv[1:2], axis=-1, keepdims=True))
               - jnp.exp(jnp.sum(lamv[2:3] * lamv[3:4], axis=-1, keepdims=True)) + lam_init)
        on = acc[:, 0:LANES] / acc[:, LANES:2 * LANES]
        o = on[0:tq] - lam * on[tq:2 * tq]
        ms = jnp.mean(o * o, axis=-1, keepdims=True)
        o = o * lax.rsqrt(ms + RMS_EPS) * g_ref[...] * (1.0 - lam_init)
        o_ref[0] = o.astype(o_ref.dtype)


def _diff_attention(qkv, lamv, norm_g, lam_init, tq, tk):
    b, s, _ = qkv.shape
    ng = tq // tk
    pairs = [(qi, ki) for qi in range(s // tq) for ki in range((qi + 1) * ng)]
    qi_tab = jnp.asarray([p[0] for p in pairs], I32)
    ki_tab = jnp.asarray([p[1] for p in pairs], I32)
    kern = functools.partial(_attn_kernel, lam_init=lam_init, tq=tq, tk=tk)
    grid_spec = pltpu.PrefetchScalarGridSpec(
        num_scalar_prefetch=2,
        grid=(b, DA_HEADS, len(pairs)),
        in_specs=[
            pl.BlockSpec((1, tq, LANES), lambda bi, h, t, qt, kt: (bi, qt[t], h)),
            pl.BlockSpec((1, tk, LANES), lambda bi, h, t, qt, kt: (bi, kt[t], DA_HEADS + h)),
            pl.BlockSpec((1, tk, LANES), lambda bi, h, t, qt, kt: (bi, kt[t], 2 * DA_HEADS + h)),
            pl.BlockSpec((8, LANES), lambda bi, h, t, qt, kt: (0, 0)),
            pl.BlockSpec((1, LANES), lambda bi, h, t, qt, kt: (0, 0)),
        ],
        out_specs=pl.BlockSpec((1, tq, LANES), lambda bi, h, t, qt, kt: (bi, qt[t], h)),
        scratch_shapes=[pltpu.VMEM((2 * tq, LANES), BF16), pltpu.VMEM((2 * tq, LANES), F32),
                        pltpu.VMEM((2 * tq, 2 * LANES), F32)],
    )
    return pl.pallas_call(
        kern,
        grid_spec=grid_spec,
        out_shape=jax.ShapeDtypeStruct((b, s, SEC), BF16),
        compiler_params=_cparams(("parallel", "parallel", "arbitrary")),
        name="diff_attn",
    )(qi_tab, ki_tab, qkv, qkv, qkv, lamv, norm_g)


def _log_sigmoid(x):
    return jnp.minimum(x, 0.0) - jnp.log1p(jnp.exp(-jnp.abs(x)))


def _sigmoid(x):
    return 1.0 / (1.0 + jnp.exp(-x))


def _mlstm_kernel(ml_ref, gt_ref, cw_ref, cb_ref, gb_ref, o_ref, xbuf, cst, mst, *, chunk):
    L = chunk
    HALO = 8
    c = pl.program_id(1)

    @pl.when(c == 0)
    def _():
        xbuf[0:HALO, :] = jnp.zeros((HALO, SEC), F32)
        cst[...] = jnp.zeros(cst.shape, F32)
        mst[...] = jnp.full(mst.shape, NEG_INF, F32)

    xbuf[HALO:HALO + L, :] = ml_ref[0, :, 0:SEC]
    y = cb_ref[...] + cw_ref[CONV_WIDTH - 1:CONV_WIDTH, :] * xbuf[HALO:HALO + L, :]
    for j in range(CONV_WIDTH - 1):
        off = HALO - (CONV_WIDTH - 1) + j
        y = y + cw_ref[j:j + 1, :] * xbuf[off:off + L, :]
    xbuf[0:HALO, :] = xbuf[L:L + HALO, :]
    qk = y * _sigmoid(y)

    gi = gt_ref[0, :, 0:LANES] + gb_ref[:, 0:LANES]
    lf = _log_sigmoid(gt_ref[0, :, LANES:GATE_W] + gb_ref[:, LANES:GATE_W])
    row = lax.broadcasted_iota(I32, (L, L), 0)
    col = lax.broadcasted_iota(I32, (L, L), 1)
    causal = row >= col
    bcum = jnp.dot(causal.astype(F32), lf, preferred_element_type=F32,
                   precision=lax.Precision.HIGHEST)
    g = bcum[L - 1:L, :]
    w_end = g - bcum + gi
    m_loc = jnp.max(w_end, axis=0, keepdims=True)
    e_end = jnp.exp(w_end - m_loc)
    m_prev = mst[...]
    m_new = jnp.maximum(g + m_prev, m_loc)
    a_dec = jnp.exp(g + m_prev - m_new)
    b_dec = jnp.exp(m_loc - m_new)
    mst[...] = m_new
    inter_log = bcum + m_prev
    bcum_t = bcum.T
    gi_t = gi.T

    lane = lax.broadcasted_iota(I32, (L, LANES), 1)
    sub = lax.broadcasted_iota(I32, (LANES, 1), 0)
    ones_col = jnp.where(lane == 0, 1.0, 0.0).astype(BF16)
    for p in range(ML_HEADS // 2):
        q_pair = qk[:, p * LANES:(p + 1) * LANES] * (ML_QK_DIM ** -0.5)
        k_pair = qk[:, 2 * LANES + p * LANES:2 * LANES + (p + 1) * LANES]
        kb = k_pair.astype(BF16)
        c_prev = cst[p].astype(BF16)
        upd = jnp.zeros((LANES, 2 * LANES), F32)
        for hh in range(2):
            h = 2 * p + hh
            head_lanes = (lane >= hh * ML_QK_DIM) & (lane < (hh + 1) * ML_QK_DIM)
            qm = jnp.where(head_lanes, q_pair, 0.0).astype(BF16)
            s = lax.dot_general(qm, kb, (((1,), (1,)), ((), ())), preferred_element_type=F32)
            d = jnp.where(causal, bcum[:, h:h + 1] - bcum_t[h:h + 1, :] + gi_t[h:h + 1, :], NEG_INF)
            il = inter_log[:, h:h + 1]
            m_t = jnp.maximum(il, jnp.max(d, axis=-1, keepdims=True))
            sd = (s * jnp.exp(d - m_t)).astype(BF16)
            inter_w = jnp.exp(il - m_t)
            v_aug = jnp.concatenate(
                [ml_ref[0, :, SEC + h * LANES:SEC + (h + 1) * LANES].astype(BF16), ones_col], axis=1)
            intra = jnp.dot(sd, v_aug, preferred_element_type=F32)
            inter = jnp.dot(qm, c_prev, preferred_element_type=F32)
            num = inter_w * inter[:, 0:LANES] + intra[:, 0:LANES]
            den = inter_w * inter[:, LANES:LANES + 1] + intra[:, LANES:LANES + 1]
            hid = num / jnp.maximum(jnp.abs(den), jnp.exp(-m_t))
            o_gate = ml_ref[0, :, 2 * SEC + h * LANES:2 * SEC + (h + 1) * LANES]
            o_ref[0, :, h * LANES:(h + 1) * LANES] = (hid * _sigmoid(o_gate)).astype(o_ref.dtype)
            ek = jnp.where(head_lanes, e_end[:, h:h + 1] * k_pair, 0.0).astype(BF16)
            upd = upd + lax.dot_general(ek, v_aug, (((0,), (0,)), ((), ())),
                                        preferred_element_type=F32)
        first = sub < ML_QK_DIM
        a_rows = jnp.where(first, a_dec[:, 2 * p:2 * p + 1], a_dec[:, 2 * p + 1:2 * p + 2])
        b_rows = jnp.where(first, b_dec[:, 2 * p:2 * p + 1], b_dec[:, 2 * p + 1:2 * p + 2])
        cst[p] = a_rows * cst[p] + b_rows * upd


def _mlstm(ml, gates, conv_w, conv_b, gate_b, chunk):
    b, s, _ = ml.shape
    kern = functools.partial(_mlstm_kernel, chunk=chunk)
    return pl.pallas_call(
        kern,
        grid=(b, s // chunk),
        in_specs=[pl.BlockSpec((1, chunk, 3 * SEC), lambda bi, c: (bi, c, 0)),
                  pl.BlockSpec((1, chunk, GATE_W), lambda bi, c: (bi, c, 0)),
                  pl.BlockSpec((CONV_WIDTH, SEC), lambda bi, c: (0, 0)),
                  pl.BlockSpec((1, SEC), lambda bi, c: (0, 0)),
                  pl.BlockSpec((1, GATE_W), lambda bi, c: (0, 0))],
        out_specs=pl.BlockSpec((1, chunk, SEC), lambda bi, c: (bi, c, 0)),
        out_shape=jax.ShapeDtypeStruct((b, s, SEC), BF16),
        scratch_shapes=[pltpu.VMEM((chunk + 8, SEC), F32),
                        pltpu.VMEM((ML_HEADS // 2, LANES, 2 * LANES), F32),
                        pltpu.VMEM((1, LANES), F32)],
        compiler_params=_cparams(("parallel", "arbitrary")),
        name="mlstm",
    )(ml, gates, conv_w, conv_b, gate_b)


SUBL = D_MODEL // LANES


def _to_tiles(ref, val):
    rows = val.shape[0]
    for s in range(SUBL):
        ref[pl.ds(s, rows, stride=SUBL), :] = val[:, s * LANES:(s + 1) * LANES]


def _tile(ref, t):
    return ref.at[pl.ds(pl.multiple_of(t * SUBL, SUBL), SUBL), :]


def _from_tiles(ref, base, rows):
    return jnp.concatenate(
        [ref[pl.ds(base * SUBL + s, rows, stride=SUBL), :] for s in range(SUBL)], axis=1)


def _layer_norm(h, g, b):
    mu = jnp.mean(h, axis=-1, keepdims=True)
    hc = h - mu
    var = jnp.mean(hc * hc, axis=-1, keepdims=True)
    return hc * lax.rsqrt(var + LN_EPS) * g + b


def _outproj_router_kernel(da_ref, mlo_ref, x_ref, wo_ref, g_ref, b_ref, wr_ref, br_ref,
                           x1_ref, x1t_ref, te_ref, tg_ref, rk_ref, cnt_ref, carry, *, tm):
    i = pl.program_id(0)

    @pl.when(i == 0)
    def _():
        carry[...] = jnp.zeros(carry.shape, F32)

    mix = (jnp.dot(da_ref[...], wo_ref[0:SEC, :], preferred_element_type=F32)
           + jnp.dot(mlo_ref[...], wo_ref[SEC:2 * SEC, :], preferred_element_type=F32))
    x1 = _layer_norm(DN_ALPHA * x_ref[...] + mix, g_ref[...], b_ref[...])
    x1_ref[...] = x1
    _to_tiles(x1t_ref, x1)

    x_hi = x1.astype(BF16)
    x_lo = (x1 - x_hi.astype(F32)).astype(BF16)
    hh_hl = jnp.dot(x_hi, wr_ref[...], preferred_element_type=F32)
    lh = jnp.dot(x_lo, wr_ref[:, 0:LANES], preferred_element_type=F32)
    logits = hh_hl[:, 0:LANES] + (hh_hl[:, LANES:2 * LANES] + lh) + br_ref[...]
    lane = lax.broadcasted_iota(I32, (tm, LANES), 1)
    lane_f = lane.astype(F32)
    work = logits
    hot = []
    vals = []
    idxs = []
    for _k in range(TOP_K):
        mx = jnp.max(work, axis=-1, keepdims=True)
        idx = jnp.min(jnp.where(work == mx, lane_f, float(LANES)), axis=-1, keepdims=True)
        sel = lane_f == idx
        hot.append(sel)
        vals.append(mx)
        idxs.append(idx)
        work = jnp.where(sel, NEG_INF, work)
    exps = [jnp.exp(v - vals[0]) for v in vals]
    inv = 1.0 / (exps[0] + exps[1] + exps[2] + exps[3])

    onehot = (hot[0] | hot[1] | hot[2] | hot[3])
    oh = jnp.where(onehot, 1.0, 0.0)
    row = lax.broadcasted_iota(I32, (tm, tm), 0)
    col = lax.broadcasted_iota(I32, (tm, tm), 1)
    before = jnp.dot((row > col).astype(BF16), oh.astype(BF16), preferred_element_type=F32)
    pos = before + carry[...]
    te = jnp.zeros((tm, LANES), I32)
    tg = jnp.zeros((tm, LANES), F32)
    rk = jnp.zeros((tm, LANES), I32)
    for k in range(TOP_K):
        slot = lane == k
        r_k = jnp.sum(jnp.where(hot[k], pos, 0.0), axis=-1, keepdims=True)
        te = jnp.where(slot, idxs[k].astype(I32), te)
        tg = jnp.where(slot, exps[k] * inv, tg)
        rk = jnp.where(slot, r_k.astype(I32), rk)
    te_ref[...] = te
    tg_ref[...] = tg
    rk_ref[...] = rk
    carry[...] = carry[...] + jnp.sum(oh, axis=0, keepdims=True)
    cnt_ref[...] = jnp.broadcast_to(carry[...], cnt_ref.shape).astype(I32)


def _outproj_router(da, mlo, x, wo, ln_g, ln_b, wr, br, tm):
    n = x.shape[0]
    kern = functools.partial(_outproj_router_kernel, tm=tm)
    row_blk = lambda w: pl.BlockSpec((tm, w), lambda i: (i, 0))
    const = lambda r, w: pl.BlockSpec((r, w), lambda i: (0, 0))
    return pl.pallas_call(
        kern,
        grid=(n // tm,),
        in_specs=[row_blk(SEC), row_blk(SEC), row_blk(D_MODEL), const(2 * SEC, D_MODEL),
                  const(1, D_MODEL), const(1, D_MODEL), const(D_MODEL, 2 * LANES), const(1, LANES)],
        out_specs=[row_blk(D_MODEL), pl.BlockSpec((tm * SUBL, LANES), lambda i: (i, 0)),
                   row_blk(LANES), row_blk(LANES), row_blk(LANES), const(8, LANES)],
        out_shape=[jax.ShapeDtypeStruct((n, D_MODEL), F32),
                   jax.ShapeDtypeStruct((n * SUBL, LANES), F32),
                   jax.ShapeDtypeStruct((n, LANES), I32),
                   jax.ShapeDtypeStruct((n, LANES), F32),
                   jax.ShapeDtypeStruct((n, LANES), I32),
                   jax.ShapeDtypeStruct((8, LANES), I32)],
        scratch_shapes=[pltpu.VMEM((1, LANES), F32)],
        compiler_params=_cparams(("arbitrary",)),
        name="outproj_router",
    )(da, mlo, x, wo, ln_g, ln_b, wr, br)


def _invert_kernel(dest_ref, lo_ref, hi_ref, inv_ref, *, ch):
    i = pl.program_id(0)
    shift = LANES.bit_length() - 1

    def put(row, val):
        inv_ref[lax.shift_right_logical(row, shift), row & (LANES - 1)] = val

    @pl.when(i == 0)
    def _():
        def per_expert(e, carry):
            def fill(r, c):
                put(r, -1)
                return c

            lax.fori_loop(lo_ref[e], hi_ref[e], fill, 0)
            return carry

        lax.fori_loop(0, N_EXPERTS, per_expert, 0)

    def body(j, carry):
        put(dest_ref[j], i * ch + j)
        return carry

    lax.fori_loop(0, ch, body, 0, unroll=8)


def _invert(dest_flat, seg_lo, seg_hi, n_rows, ch):
    return pl.pallas_call(
        functools.partial(_invert_kernel, ch=ch),
        grid=(dest_flat.shape[0] // ch,),
        in_specs=[pl.BlockSpec((ch,), lambda i: (i,), memory_space=pltpu.SMEM),
                  pl.BlockSpec(memory_space=pltpu.SMEM),
                  pl.BlockSpec(memory_space=pltpu.SMEM)],
        out_specs=pl.BlockSpec(memory_space=pltpu.SMEM),
        out_shape=jax.ShapeDtypeStruct((n_rows // LANES, LANES), I32),
        compiler_params=_cparams(("arbitrary",)),
        name="moe_invert",
    )(dest_flat, seg_lo, seg_hi)


FF_CHUNK = 256


def _moe_kernel(be_ref, nused_ref, invp_ref, invn_ref, x1t_ref, wgu_ref, bgu_ref, wd_ref, bd_ref,
                out_ref, wgu_b, wd_b, xbuf, ybuf, gsem, ssem, *, tm, n_slots):
    i = pl.program_id(0)
    nused = nused_ref[0]
    live = i < nused
    fresh = jnp.logical_or(i == 0, be_ref[i] != be_ref[jnp.maximum(i - 1, 0)])
    rows_all = pl.ds(0, tm * SUBL)
    lane_shift = LANES.bit_length() - 1

    def gather(buf, sem, r, tokk):
        tok = lax.shift_right_logical(jnp.maximum(tokk, 0), TOP_K.bit_length() - 1)
        return pltpu.make_async_copy(_tile(x1t_ref, tok), _tile(buf, r), sem)

    def scatter(buf, sem, r, tokk, to_dump):
        slot = jnp.where(jnp.logical_or(to_dump, tokk < 0), n_slots + r, tokk)
        return pltpu.make_async_copy(_tile(buf, r), _tile(out_ref, slot), sem)

    def gather_wait(buf, sem):
        pltpu.make_async_copy(x1t_ref.at[rows_all, :], buf, sem).wait()

    def scatter_wait(buf, sem):
        pltpu.make_async_copy(buf, out_ref.at[rows_all, :], sem).wait()

    def idx(ref, r):
        return ref[0, r // LANES, r % LANES]

    def idx_dyn(ref, r):
        return ref[0, lax.shift_right_logical(r, lane_shift), r & (LANES - 1)]

    @pl.when(i == 0)
    def _():
        ybuf[1] = jnp.zeros(ybuf.shape[1:], F32)
        fill = pltpu.make_async_copy(ybuf.at[1], out_ref.at[pl.ds(n_slots * SUBL, tm * SUBL), :],
                                     ssem.at[1])
        fill.start()

        def first_rows(r, carry):
            gather(xbuf.at[0], gsem.at[0], r, idx_dyn(invp_ref, r)).start()
            return carry

        lax.fori_loop(0, tm, first_rows, 0)
        fill.wait()

    @pl.when(jnp.logical_and(live, fresh))
    def _():
        wgu_b[...] = wgu_ref[0].astype(BF16)
        wd_b[...] = wd_ref[0].astype(BF16)

    @pl.when(live)
    def _():
        p = i % 2
        xb, xo, yb, yo = xbuf.at[p], xbuf.at[1 - p], ybuf.at[p], ybuf.at[1 - p]
        gather_wait(xb, gsem.at[p])
        x16 = _from_tiles(xb, 0, tm).astype(BF16)
        first = i == 0
        n_chunks = D_FF // FF_CHUNK
        per = tm // n_chunks
        y = None
        for c in range(n_chunks):
            for r in range(c * per, (c + 1) * per):
                gather(xo, gsem.at[1 - p], r, idx(invn_ref, r)).start(priority=0)
                scatter(yo, ssem.at[1 - p], r, idx(invp_ref, r), first).start(priority=1)
            lo, hi = c * FF_CHUNK, (c + 1) * FF_CHUNK
            hg = jnp.dot(x16, wgu_b[:, lo:hi], preferred_element_type=F32) + bgu_ref[0, :, lo:hi]
            hu = (jnp.dot(x16, wgu_b[:, D_FF + lo:D_FF + hi], preferred_element_type=F32)
                  + bgu_ref[0, :, D_FF + lo:D_FF + hi])
            gate = jnp.minimum(hg, SWIGLU_LIMIT)
            up = jnp.clip(hu, -SWIGLU_LIMIT, SWIGLU_LIMIT)
            act = ((up + 1.0) * (gate * _sigmoid(SWIGLU_ALPHA * gate))).astype(BF16)
            part = jnp.dot(act, wd_b[lo:hi, :], preferred_element_type=F32)
            y = part if y is None else y + part
        _to_tiles(yb, y + bd_ref[0])
        scatter_wait(yo, ssem.at[1 - p])

        @pl.when(i == nused - 1)
        def _():
            gather_wait(xo, gsem.at[1 - p])

            def last_rows(r, carry):
                scatter(yb, ssem.at[p], r, idx_dyn(invn_ref, r), False).start()
                return carry

            lax.fori_loop(0, tm, last_rows, 0)
            scatter_wait(yb, ssem.at[p])


def _experts(x1t, inv, block_e, nused, wgu, bgu, wd, bd, layer, tm):
    n_slots = x1t.shape[0] // SUBL * TOP_K
    nb = inv.shape[0] * LANES // tm
    inv3 = inv.reshape(nb, tm // LANES, LANES)

    def per_e(i, be, nu):
        return (layer * N_EXPERTS + be[i], 0, 0)

    grid_spec = pltpu.PrefetchScalarGridSpec(
        num_scalar_prefetch=2,
        grid=(nb,),
        in_specs=[pl.BlockSpec((1, tm // LANES, LANES), lambda i, be, nu: (jnp.maximum(i - 1, 0), 0, 0),
                               memory_space=pltpu.SMEM),
                  pl.BlockSpec((1, tm // LANES, LANES),
                               lambda i, be, nu: (jnp.minimum(i + 1, nu[0] - 1), 0, 0),
                               memory_space=pltpu.SMEM),
                  pl.BlockSpec(memory_space=pl.ANY),
                  pl.BlockSpec((1, D_MODEL, 2 * D_FF), per_e),
                  pl.BlockSpec((1, 1, 2 * D_FF), per_e),
                  pl.BlockSpec((1, D_FF, D_MODEL), per_e),
                  pl.BlockSpec((1, 1, D_MODEL), per_e)],
        out_specs=pl.BlockSpec(memory_space=pl.ANY),
        scratch_shapes=[pltpu.VMEM((D_MODEL, 2 * D_FF), BF16), pltpu.VMEM((D_FF, D_MODEL), BF16),
                        pltpu.VMEM((2, tm * SUBL, LANES), F32), pltpu.VMEM((2, tm * SUBL, LANES), F32),
                        pltpu.SemaphoreType.DMA((2,)), pltpu.SemaphoreType.DMA((2,))],
    )
    return pl.pallas_call(
        functools.partial(_moe_kernel, tm=tm, n_slots=n_slots),
        grid_spec=grid_spec,
        out_shape=jax.ShapeDtypeStruct(((n_slots + tm) * SUBL, LANES), F32),
        compiler_params=_cparams(("arbitrary",)),
        name="moe_experts",
    )(block_e, nused, inv3, inv3, x1t, wgu, bgu, wd, bd)


def _combine_kernel(o4_ref, x1_ref, tg_ref, g_ref, b_ref, o_ref, *, tm):
    tg = tg_ref[...]
    y = None
    for k in range(TOP_K):
        rows = jnp.concatenate(
            [o4_ref[pl.ds(k * SUBL + s, tm, stride=TOP_K * SUBL), :] for s in range(SUBL)], axis=1)
        term = tg[:, k:k + 1] * rows
        y = term if y is None else y + term
    o_ref[...] = _layer_norm(DN_ALPHA * x1_ref[...] + y, g_ref[...], b_ref[...])


def _combine(o4, x1, tg, ln_g, ln_b, tm):
    n = x1.shape[0]
    kern = functools.partial(_combine_kernel, tm=tm)
    return pl.pallas_call(
        kern,
        grid=(n // tm,),
        in_specs=[pl.BlockSpec((tm * TOP_K * SUBL, LANES), lambda i: (i, 0)),
                  pl.BlockSpec((tm, D_MODEL), lambda i: (i, 0)),
                  pl.BlockSpec((tm, LANES), lambda i: (i, 0)),
                  pl.BlockSpec((1, D_MODEL), lambda i: (0, 0)),
                  pl.BlockSpec((1, D_MODEL), lambda i: (0, 0))],
        out_specs=pl.BlockSpec((tm, D_MODEL), lambda i: (i, 0)),
        out_shape=jax.ShapeDtypeStruct((n, D_MODEL), F32),
        compiler_params=_cparams(("parallel",)),
        name="moe_combine",
    )(o4, x1, tg, ln_g, ln_b)


def _pad_lanes(v, width=LANES, value=0.0):
    return jnp.pad(v, ((0, 0), (0, width - v.shape[-1])), constant_values=value)


def _prep_w_in(w):
    body = w[:, :6 * SEC]
    gi = _pad_lanes(w[:, 6 * SEC:6 * SEC + ML_HEADS])
    gf = _pad_lanes(w[:, 6 * SEC + ML_HEADS:6 * SEC + 2 * ML_HEADS])
    return jnp.concatenate([body, gi, gf], axis=1).astype(BF16)


def _layer(x, p, big, layer, lam_init, cfg):
    n = x.shape[0]
    b, s = cfg["batch"], cfg["seq"]
    tm_e = cfg["tm_expert"]

    qkv, ml, gates = _inproj(x, _prep_w_in(p["w_in"]), cfg["tm_proj"])

    lamv = jnp.zeros((8, LANES), F32)
    for r, nm in enumerate(("lam_q1", "lam_k1", "lam_q2", "lam_k2")):
        lamv = lamv.at[r, :DA_QK_DIM].set(p[nm])
    da = _diff_attention(qkv.reshape(b, s, 3 * SEC), lamv, p["da_norm_g"].reshape(1, LANES),
                         lam_init, cfg["tq"], cfg["tk"])

    gate_b = jnp.concatenate([_pad_lanes(p["gate_b"][None, :ML_HEADS]),
                              _pad_lanes(p["gate_b"][None, ML_HEADS:])], axis=1)
    mlo = _mlstm(ml.reshape(b, s, 3 * SEC), gates.reshape(b, s, GATE_W), p["conv_w"],
                 p["conv_b"].reshape(1, SEC), gate_b, cfg["chunk"])

    wr = _pad_lanes(p["w_router"])
    wr_hi = wr.astype(BF16)
    wr = jnp.concatenate([wr_hi, (wr - wr_hi.astype(F32)).astype(BF16)], axis=1)
    br = _pad_lanes(p["b_router"][None, :], value=NEG_INF)
    x1, x1t, te, tg, rk, cnt = _outproj_router(
        da.reshape(n, SEC), mlo.reshape(n, SEC), x, p["w_out"].astype(BF16),
        p["ln1_g"].reshape(1, D_MODEL), p["ln1_b"].reshape(1, D_MODEL), wr, br, cfg["tm_proj"])

    counts = cnt[0, :N_EXPERTS]
    padded = (counts + tm_e - 1) // tm_e * tm_e
    pad_ends = jnp.cumsum(padded)
    pad_starts = pad_ends - padded
    top_e = te[:, :TOP_K]
    sel = top_e[:, :, None] == jnp.arange(N_EXPERTS, dtype=I32)[None, None, :]
    dest = rk[:, :TOP_K] + jnp.sum(jnp.where(sel, pad_starts[None, None, :], 0), axis=-1)
    dest_flat = dest.reshape(n * TOP_K).astype(I32)
    n_rows = n * TOP_K + N_EXPERTS * tm_e
    nb = n_rows // tm_e
    blk_start = jnp.arange(nb, dtype=I32) * tm_e
    block_e = jnp.minimum(jnp.sum(blk_start[:, None] >= pad_ends[None, :], axis=-1),
                          N_EXPERTS - 1).astype(I32)
    nused = (pad_ends[-1:] // tm_e).astype(I32)
    seg_lo = (pad_starts + counts).astype(I32)
    seg_hi = pad_ends.at[N_EXPERTS - 1].set(n_rows).astype(I32)

    inv = _invert(dest_flat, seg_lo, seg_hi, n_rows, cfg["inv_chunk"])
    o4 = _experts(x1t, inv, block_e, nused, big["w_gu"], big["b_gu"], big["w_down"], big["b_down"],
                  layer, tm_e)
    return _combine(o4, x1, tg, p["ln2_g"].reshape(1, D_MODEL), p["ln2_b"].reshape(1, D_MODEL),
                    cfg["tm_comb"])


def _forward(x, params, cfg):
    b, s, d = x.shape
    h = x.reshape(b * s, d)
    depth = params["w_in"].shape[0]
    big = dict(w_gu=params["w_gu"].reshape(depth * N_EXPERTS, D_MODEL, 2 * D_FF),
               b_gu=params["b_gu"].reshape(depth * N_EXPERTS, 1, 2 * D_FF),
               w_down=params["w_down"].reshape(depth * N_EXPERTS, D_FF, D_MODEL),
               b_down=params["b_down"].reshape(depth * N_EXPERTS, 1, D_MODEL))
    for l in range(depth):
        lam_init = 0.8 - 0.6 * math.exp(-0.3 * l)
        p = {k: v[l] for k, v in params.items() if k not in big}
        h = _layer(h, p, big, l, lam_init, cfg)
    return h.reshape(b, s, d)


def kernel(x, w_in, conv_w, conv_b, gate_b, lam_q1, lam_k1, lam_q2, lam_k2, da_norm_g, w_out,
           ln1_g, ln1_b, w_router, b_router, w_gu, b_gu, w_down, b_down, ln2_g, ln2_b):
    params = dict(w_in=w_in, conv_w=conv_w, conv_b=conv_b, gate_b=gate_b, lam_q1=lam_q1,
                  lam_k1=lam_k1, lam_q2=lam_q2, lam_k2=lam_k2, da_norm_g=da_norm_g, w_out=w_out,
                  ln1_g=ln1_g, ln1_b=ln1_b, w_router=w_router, b_router=b_router, w_gu=w_gu,
                  b_gu=b_gu, w_down=w_down, b_down=b_down, ln2_g=ln2_g, ln2_b=ln2_b)
    cfg = dict(batch=x.shape[0], seq=x.shape[1], tm_proj=512, tq=2048, tk=512, chunk=256,
               tm_expert=512, inv_chunk=8192, tm_comb=256)
    return _forward(x, params, cfg)
```

```python
import functools
import math

import jax
import jax.numpy as jnp
from jax import lax
from jax.experimental import pallas as pl
from jax.experimental.pallas import tpu as pltpu

F32 = jnp.float32
BF16 = jnp.bfloat16
I32 = jnp.int32

D_MODEL = 1024
DEPTH = 4
DA_HEADS = 4
DA_QK_DIM = 64
DA_V_DIM = 128
ML_HEADS = 4
ML_QK_DIM = 64
ML_V_DIM = 128
CONV_WIDTH = 4
N_EXPERTS = 32
TOP_K = 4
D_FF = 1024
SWIGLU_LIMIT = 7.0
SWIGLU_ALPHA = 1.702
DN_ALPHA = (2 * DEPTH) ** 0.25
LN_EPS = 1e-5
RMS_EPS = 1e-5

LANES = 128
SEC = 512
GATE_W = 2 * LANES
IN_W_PAD = 6 * SEC + GATE_W
VMEM_LIMIT = 56 * 1024 * 1024
NEG_INF = float("-inf")


def _cparams(sem):
    return pltpu.CompilerParams(dimension_semantics=sem, vmem_limit_bytes=VMEM_LIMIT)


def _inproj_kernel(x_ref, w_ref, qkv_ref, ml_ref, gate_ref):
    xb = x_ref[...].astype(BF16)
    for c in range(3):
        sl = slice(c * SEC, (c + 1) * SEC)
        qkv_ref[:, sl] = jnp.dot(xb, w_ref[:, sl], preferred_element_type=F32).astype(BF16)
    for c in range(3):
        ml_ref[:, c * SEC:(c + 1) * SEC] = jnp.dot(
            xb, w_ref[:, (3 + c) * SEC:(4 + c) * SEC], preferred_element_type=F32)
    gate_ref[...] = jnp.dot(xb, w_ref[:, 6 * SEC:], preferred_element_type=F32)


def _inproj(x, w, tm):
    n = x.shape[0]
    return pl.pallas_call(
        _inproj_kernel,
        grid=(n // tm,),
        in_specs=[pl.BlockSpec((tm, D_MODEL), lambda i: (i, 0)),
                  pl.BlockSpec((D_MODEL, IN_W_PAD), lambda i: (0, 0))],
        out_specs=[pl.BlockSpec((tm, 3 * SEC), lambda i: (i, 0)),
                   pl.BlockSpec((tm, 3 * SEC), lambda i: (i, 0)),
                   pl.BlockSpec((tm, GATE_W), lambda i: (i, 0))],
        out_shape=[jax.ShapeDtypeStruct((n, 3 * SEC), BF16),
                   jax.ShapeDtypeStruct((n, 3 * SEC), F32),
                   jax.ShapeDtypeStruct((n, GATE_W), F32)],
        compiler_params=_cparams(("parallel",)),
        name="in_proj",
    )(x, w)


def _attn_kernel(qi_tab, ki_tab, q_ref, k_ref, v_ref, lamv_ref, g_ref, o_ref, qs, m_s, acc,
                 *, lam_init, tq, tk):
    step = pl.program_id(2)
    qi = qi_tab[step]
    ki = ki_tab[step]
    ng = tq // tk
    reps = tk // LANES

    @pl.when(ki == 0)
    def _():
        q = q_ref[0] * jnp.asarray(DA_QK_DIM ** -0.5, BF16)
        lane = lax.broadcasted_iota(I32, q.shape, 1)
        zero = jnp.zeros_like(q)
        qs[0:tq, :] = jnp.where(lane < DA_QK_DIM, q, zero)
        qs[tq:2 * tq, :] = jnp.where(lane >= DA_QK_DIM, q, zero)
        m_s[...] = jnp.full(m_s.shape, NEG_INF, F32)
        acc[...] = jnp.zeros(acc.shape, F32)

    def update(diag):
        k = k_ref[0]
        v_aug = jnp.concatenate([v_ref[0], jnp.ones((tk, LANES), BF16)], axis=1)
        groups = [(gi, slice(mp * tq + gi * tk, mp * tq + (gi + 1) * tk))
                  for gi in range(ng) if gi >= diag for mp in range(2)]
        ss = [lax.dot_general(qs[g, :], k, (((1,), (1,)), ((), ())), preferred_element_type=F32)
              for _, g in groups]
        if diag >= 0:
            row = lax.broadcasted_iota(I32, (tk, tk), 0)
            col = lax.broadcasted_iota(I32, (tk, tk), 1)
            ss = [jnp.where(row >= col, s, NEG_INF) if gi == diag else s
                  for (gi, _), s in zip(groups, ss)]
        for (_, g), s in zip(groups, ss):
            m_old = m_s[g, :]
            m_new = jnp.maximum(m_old, jnp.max(s, axis=-1, keepdims=True))
            p = jnp.exp(s - jnp.tile(m_new, (1, reps)))
            alpha = jnp.exp(m_old - m_new)
            pv = jnp.dot(p.astype(BF16), v_aug, preferred_element_type=F32)
            acc[g, :] = jnp.tile(alpha, (1, 2)) * acc[g, :] + pv
            m_s[g, :] = m_new

    @pl.when(ki < qi * ng)
    def _():
        update(-1)

    for d in range(ng):
        @pl.when(ki == qi * ng + d)
        def _(d=d):
            update(d)

    @pl.when(ki == qi * ng + ng - 1)
    def _():
        lamv = lamv_ref[...]
        lam = (jnp.exp(jnp.sum(lamv[0:1] * lamv[1:2], axis=-1, keepdims=True))
               - jnp.exp(jnp.sum(lamv[2:3] * lamv[3:4], axis=-1, keepdims=True)) + lam_init)
        on = acc[:, 0:LANES] / acc[:, LANES:2 * LANES]
        o = on[0:tq] - lam * on[tq:2 * tq]
        ms = jnp.mean(o * o, axis=-1, keepdims=True)
        o = o * lax.rsqrt(ms + RMS_EPS) * g_ref[...] * (1.0 - lam_init)
        o_ref[0] = o.astype(o_ref.dtype)


def _diff_attention(qkv, lamv, norm_g, lam_init, tq, tk):
    b, s, _ = qkv.shape
    ng = tq // tk
    pairs = [(qi, ki) for qi in range(s // tq) for ki in range((qi + 1) * ng)]
    qi_tab = jnp.asarray([p[0] for p in pairs], I32)
    ki_tab = jnp.asarray([p[1] for p in pairs], I32)
    kern = functools.partial(_attn_kernel, lam_init=lam_init, tq=tq, tk=tk)
    grid_spec = pltpu.PrefetchScalarGridSpec(
        num_scalar_prefetch=2,
        grid=(b, DA_HEADS, len(pairs)),
        in_specs=[
            pl.BlockSpec((1, tq, LANES), lambda bi, h, t, qt, kt: (bi, qt[t], h)),
            pl.BlockSpec((1, tk, LANES), lambda bi, h, t, qt, kt: (bi, kt[t], DA_HEADS + h)),
            pl.BlockSpec((1, tk, LANES), lambda bi, h, t, qt, kt: (bi, kt[t], 2 * DA_HEADS + h)),
            pl.BlockSpec((8, LANES), lambda bi, h, t, qt, kt: (0, 0)),
            pl.BlockSpec((1, LANES), lambda bi, h, t, qt, kt: (0, 0)),
        ],
        out_specs=pl.BlockSpec((1, tq, LANES), lambda bi, h, t, qt, kt: (bi, qt[t], h)),
        scratch_shapes=[pltpu.VMEM((2 * tq, LANES), BF16), pltpu.VMEM((2 * tq, LANES), F32),
                        pltpu.VMEM((2 * tq, 2 * LANES), F32)],
    )
    return pl.pallas_call(
        kern,
        grid_spec=grid_spec,
        out_shape=jax.ShapeDtypeStruct((b, s, SEC), BF16),
        compiler_params=_cparams(("parallel", "parallel", "arbitrary")),
        name="diff_attn",
    )(qi_tab, ki_tab, qkv, qkv, qkv, lamv, norm_g)


def _log_sigmoid(x):
    return jnp.minimum(x, 0.0) - jnp.log1p(jnp.exp(-jnp.abs(x)))


def _sigmoid(x):
    return 1.0 / (1.0 + jnp.exp(-x))


def _mlstm_kernel(ml_ref, gt_ref, cw_ref, cb_ref, gb_ref, o_ref, xbuf, cst, mst, *, chunk, group):
    @pl.when(pl.program_id(1) == 0)
    def _():
        xbuf[:, 0:ML_HALO, :] = jnp.zeros((group, ML_HALO, SEC), F32)
        cst[...] = jnp.zeros(cst.shape, F32)
        mst[...] = jnp.full(mst.shape, NEG_INF, F32)

    for bb in range(group):
        _mlstm_chunk(bb, ml_ref, gt_ref, cw_ref, cb_ref, gb_ref, o_ref, xbuf, cst, mst, chunk)


ML_HALO = 8


def _mlstm_chunk(bb, ml_ref, gt_ref, cw_ref, cb_ref, gb_ref, o_ref, xbuf, cst, mst, chunk):
    L = chunk
    HALO = ML_HALO

    xbuf[bb, HALO:HALO + L, :] = ml_ref[bb, :, 0:SEC]
    y = cb_ref[...] + cw_ref[CONV_WIDTH - 1:CONV_WIDTH, :] * xbuf[bb, HALO:HALO + L, :]
    for j in range(CONV_WIDTH - 1):
        off = HALO - (CONV_WIDTH - 1) + j
        y = y + cw_ref[j:j + 1, :] * xbuf[bb, off:off + L, :]
    xbuf[bb, 0:HALO, :] = xbuf[bb, L:L + HALO, :]
    qk = y * _sigmoid(y)

    gi = gt_ref[bb, :, 0:LANES] + gb_ref[:, 0:LANES]
    lf = _log_sigmoid(gt_ref[bb, :, LANES:GATE_W] + gb_ref[:, LANES:GATE_W])
    row = lax.broadcasted_iota(I32, (L, L), 0)
    col = lax.broadcasted_iota(I32, (L, L), 1)
    causal = row >= col
    bcum = jnp.dot(causal.astype(F32), lf, preferred_element_type=F32,
                   precision=lax.Precision.HIGHEST)
    g = bcum[L - 1:L, :]
    w_end = g - bcum + gi
    m_loc = jnp.max(w_end, axis=0, keepdims=True)
    e_end = jnp.exp(w_end - m_loc)
    m_prev = mst[bb]
    m_new = jnp.maximum(g + m_prev, m_loc)
    a_dec = jnp.exp(g + m_prev - m_new)
    b_dec = jnp.exp(m_loc - m_new)
    mst[bb] = m_new
    inter_log = bcum + m_prev
    bcum_t = bcum.T
    gi_t = gi.T

    lane = lax.broadcasted_iota(I32, (L, LANES), 1)
    sub = lax.broadcasted_iota(I32, (LANES, 1), 0)
    ones_col = jnp.where(lane == 0, 1.0, 0.0).astype(BF16)
    for p in range(ML_HEADS // 2):
        q_pair = qk[:, p * LANES:(p + 1) * LANES] * (ML_QK_DIM ** -0.5)
        k_pair = qk[:, 2 * LANES + p * LANES:2 * LANES + (p + 1) * LANES]
        kb = k_pair.astype(BF16)
        cp = bb * (ML_HEADS // 2) + p
        c_prev = cst[cp].astype(BF16)
        upd = jnp.zeros((LANES, 2 * LANES), F32)
        for hh in range(2):
            h = 2 * p + hh
            head_lanes = (lane >= hh * ML_QK_DIM) & (lane < (hh + 1) * ML_QK_DIM)
            qm = jnp.where(head_lanes, q_pair, 0.0).astype(BF16)
            s = lax.dot_general(qm, kb, (((1,), (1,)), ((), ())), preferred_element_type=F32)
            d = jnp.where(causal, bcum[:, h:h + 1] - bcum_t[h:h + 1, :] + gi_t[h:h + 1, :], NEG_INF)
            il = inter_log[:, h:h + 1]
            m_t = jnp.maximum(il, jnp.max(d, axis=-1, keepdims=True))
            sd = (s * jnp.exp(d - m_t)).astype(BF16)
            inter_w = jnp.exp(il - m_t)
            v_aug = jnp.concatenate(
                [ml_ref[bb, :, SEC + h * LANES:SEC + (h + 1) * LANES].astype(BF16), ones_col], axis=1)
            intra = jnp.dot(sd, v_aug, preferred_element_type=F32)
            inter = jnp.dot(qm, c_prev, preferred_element_type=F32)
            num = inter_w * inter[:, 0:LANES] + intra[:, 0:LANES]
            den = inter_w * inter[:, LANES:LANES + 1] + intra[:, LANES:LANES + 1]
            hid = num / jnp.maximum(jnp.abs(den), jnp.exp(-m_t))
            o_gate = ml_ref[bb, :, 2 * SEC + h * LANES:2 * SEC + (h + 1) * LANES]
            o_ref[bb, :, h * LANES:(h + 1) * LANES] = (hid * _sigmoid(o_gate)).astype(o_ref.dtype)
            ek = jnp.where(head_lanes, e_end[:, h:h + 1] * k_pair, 0.0).astype(BF16)
            upd = upd + lax.dot_general(ek, v_aug, (((0,), (0,)), ((), ())),
                                        preferred_element_type=F32)
        first = sub < ML_QK_DIM
        a_rows = jnp.where(first, a_dec[:, 2 * p:2 * p + 1], a_dec[:, 2 * p + 1:2 * p + 2])
        b_rows = jnp.where(first, b_dec[:, 2 * p:2 * p + 1], b_dec[:, 2 * p + 1:2 * p + 2])
        cst[cp] = a_rows * cst[cp] + b_rows * upd


def _mlstm(ml, gates, conv_w, conv_b, gate_b, chunk, group):
    b, s, _ = ml.shape
    kern = functools.partial(_mlstm_kernel, chunk=chunk, group=group)
    return pl.pallas_call(
        kern,
        grid=(b // group, s // chunk),
        in_specs=[pl.BlockSpec((group, chunk, 3 * SEC), lambda bi, c: (bi, c, 0)),
                  pl.BlockSpec((group, chunk, GATE_W), lambda bi, c: (bi, c, 0)),
                  pl.BlockSpec((CONV_WIDTH, SEC), lambda bi, c: (0, 0)),
                  pl.BlockSpec((1, SEC), lambda bi, c: (0, 0)),
                  pl.BlockSpec((1, GATE_W), lambda bi, c: (0, 0))],
        out_specs=pl.BlockSpec((group, chunk, SEC), lambda bi, c: (bi, c, 0)),
        out_shape=jax.ShapeDtypeStruct((b, s, SEC), BF16),
        scratch_shapes=[pltpu.VMEM((group, chunk + 8, SEC), F32),
                        pltpu.VMEM((group * (ML_HEADS // 2), LANES, 2 * LANES), F32),
                        pltpu.VMEM((group, 1, LANES), F32)],
        compiler_params=_cparams(("parallel", "arbitrary")),
        name="mlstm",
    )(ml, gates, conv_w, conv_b, gate_b)


SUBL = D_MODEL // LANES


def _to_tiles(ref, val, base=0):
    rows = val.shape[0]
    for s in range(SUBL):
        ref[pl.ds(base * SUBL + s, rows, stride=SUBL), :] = val[:, s * LANES:(s + 1) * LANES]


def _tile(ref, t):
    return ref.at[pl.ds(pl.multiple_of(t * SUBL, SUBL), SUBL), :]


def _from_tiles(ref, base, rows):
    return jnp.concatenate(
        [ref[pl.ds(base * SUBL + s, rows, stride=SUBL), :] for s in range(SUBL)], axis=1)


def _layer_norm(h, g, b):
    mu = jnp.mean(h, axis=-1, keepdims=True)
    hc = h - mu
    var = jnp.mean(hc * hc, axis=-1, keepdims=True)
    return hc * lax.rsqrt(var + LN_EPS) * g + b


def _outproj_router_kernel(da_ref, mlo_ref, x_ref, wo_ref, g_ref, b_ref, wr_ref, br_ref,
                           x1_ref, x1t_ref, te_ref, tg_ref, rk_ref, cnt_ref, carry, *, tm):
    i = pl.program_id(0)

    @pl.when(i == 0)
    def _():
        carry[...] = jnp.zeros(carry.shape, F32)

    mix = (jnp.dot(da_ref[...], wo_ref[0:SEC, :], preferred_element_type=F32)
           + jnp.dot(mlo_ref[...], wo_ref[SEC:2 * SEC, :], preferred_element_type=F32))
    x1 = _layer_norm(DN_ALPHA * x_ref[...] + mix, g_ref[...], b_ref[...])
    x1_ref[...] = x1
    _to_tiles(x1t_ref, x1)

    x_hi = x1.astype(BF16)
    x_lo = (x1 - x_hi.astype(F32)).astype(BF16)
    hh_hl = jnp.dot(x_hi, wr_ref[...], preferred_element_type=F32)
    lh = jnp.dot(x_lo, wr_ref[:, 0:LANES], preferred_element_type=F32)
    logits = hh_hl[:, 0:LANES] + (hh_hl[:, LANES:2 * LANES] + lh) + br_ref[...]
    lane = lax.broadcasted_iota(I32, (tm, LANES), 1)
    lane_f = lane.astype(F32)
    work = logits
    hot = []
    vals = []
    idxs = []
    for _k in range(TOP_K):
        mx = jnp.max(work, axis=-1, keepdims=True)
        idx = jnp.min(jnp.where(work == mx, lane_f, float(LANES)), axis=-1, keepdims=True)
        sel = lane_f == idx
        hot.append(sel)
        vals.append(mx)
        idxs.append(idx)
        work = jnp.where(sel, NEG_INF, work)
    exps = [jnp.exp(v - vals[0]) for v in vals]
    inv = 1.0 / (exps[0] + exps[1] + exps[2] + exps[3])

    onehot = (hot[0] | hot[1] | hot[2] | hot[3])
    oh = jnp.where(onehot, 1.0, 0.0)
    row = lax.broadcasted_iota(I32, (tm, tm), 0)
    col = lax.broadcasted_iota(I32, (tm, tm), 1)
    before = jnp.dot((row > col).astype(BF16), oh.astype(BF16), preferred_element_type=F32)
    pos = before + carry[...]
    te = jnp.zeros((tm, LANES), I32)
    tg = jnp.zeros((tm, LANES), F32)
    rk = jnp.zeros((tm, LANES), I32)
    for k in range(TOP_K):
        slot = lane == k
        r_k = jnp.sum(jnp.where(hot[k], pos, 0.0), axis=-1, keepdims=True)
        te = jnp.where(slot, idxs[k].astype(I32), te)
        tg = jnp.where(slot, exps[k] * inv, tg)
        rk = jnp.where(slot, r_k.astype(I32), rk)
    te_ref[...] = te
    tg_ref[...] = tg
    rk_ref[...] = rk
    carry[...] = carry[...] + jnp.sum(oh, axis=0, keepdims=True)
    cnt_ref[...] = jnp.broadcast_to(carry[...], cnt_ref.shape).astype(I32)


def _outproj_router(da, mlo, x, wo, ln_g, ln_b, wr, br, tm):
    n = x.shape[0]
    kern = functools.partial(_outproj_router_kernel, tm=tm)
    row_blk = lambda w: pl.BlockSpec((tm, w), lambda i: (i, 0))
    const = lambda r, w: pl.BlockSpec((r, w), lambda i: (0, 0))
    return pl.pallas_call(
        kern,
        grid=(n // tm,),
        in_specs=[row_blk(SEC), row_blk(SEC), row_blk(D_MODEL), const(2 * SEC, D_MODEL),
                  const(1, D_MODEL), const(1, D_MODEL), const(D_MODEL, 2 * LANES), const(1, LANES)],
        out_specs=[row_blk(D_MODEL), pl.BlockSpec((tm * SUBL, LANES), lambda i: (i, 0)),
                   row_blk(LANES), row_blk(LANES), row_blk(LANES), const(8, LANES)],
        out_shape=[jax.ShapeDtypeStruct((n, D_MODEL), F32),
                   jax.ShapeDtypeStruct((n * SUBL, LANES), F32),
                   jax.ShapeDtypeStruct((n, LANES), I32),
                   jax.ShapeDtypeStruct((n, LANES), F32),
                   jax.ShapeDtypeStruct((n, LANES), I32),
                   jax.ShapeDtypeStruct((8, LANES), I32)],
        scratch_shapes=[pltpu.VMEM((1, LANES), F32)],
        compiler_params=_cparams(("arbitrary",)),
        name="outproj_router",
    )(da, mlo, x, wo, ln_g, ln_b, wr, br)


def _dispatch_kernel(dest_ref, lo_ref, hi_ref, x_ref, xs_ref, zeros, sem, zsem, *, tm):
    i = pl.program_id(0)

    def issue(t, carry):
        for k in range(TOP_K):
            pltpu.make_async_copy(_tile(x_ref, t), _tile(xs_ref, dest_ref[t * TOP_K + k]),
                                  sem).start(priority=k % 2)
        return carry

    lax.fori_loop(0, tm, issue, 0)

    @pl.when(i == 0)
    def _():
        zeros[...] = jnp.zeros(zeros.shape, F32)

        def per_expert(e, carry):
            def fill(r, c):
                pltpu.make_async_copy(zeros, _tile(xs_ref, r), zsem).start()
                return c

            def drain(r, c):
                pltpu.make_async_copy(zeros, _tile(xs_ref, r), zsem).wait()
                return c

            lax.fori_loop(lo_ref[e], hi_ref[e], fill, 0)
            lax.fori_loop(lo_ref[e], hi_ref[e], drain, 0)
            return carry

        lax.fori_loop(0, N_EXPERTS, per_expert, 0)

    for k in range(TOP_K):
        pltpu.make_async_copy(x_ref, xs_ref.at[pl.ds(0, tm * SUBL), :], sem).wait()


def _dispatch(x1t, dest_flat, seg_lo, seg_hi, n_rows, tm):
    n = x1t.shape[0] // SUBL
    kern = functools.partial(_dispatch_kernel, tm=tm)
    return pl.pallas_call(
        kern,
        grid=(n // tm,),
        in_specs=[pl.BlockSpec((tm * TOP_K,), lambda i: (i,), memory_space=pltpu.SMEM),
                  pl.BlockSpec(memory_space=pltpu.SMEM),
                  pl.BlockSpec(memory_space=pltpu.SMEM),
                  pl.BlockSpec((tm * SUBL, LANES), lambda i: (i, 0))],
        out_specs=pl.BlockSpec(memory_space=pl.ANY),
        out_shape=jax.ShapeDtypeStruct((n_rows * SUBL, LANES), F32),
        scratch_shapes=[pltpu.VMEM((SUBL, LANES), F32), pltpu.SemaphoreType.DMA(()),
                        pltpu.SemaphoreType.DMA(())],
        compiler_params=_cparams(("arbitrary",)),
        name="moe_dispatch",
    )(dest_flat, seg_lo, seg_hi, x1t)


FF_CHUNK = 256


def _expert_kernel(be_ref, nused_ref, xs_ref, wgu_ref, bgu_ref, wd_ref, bd_ref, ys_ref,
                   wgu_b, wd_b, *, tm):
    i = pl.program_id(0)
    live = i < nused_ref[0]
    fresh = jnp.logical_or(i == 0, be_ref[i] != be_ref[jnp.maximum(i - 1, 0)])

    @pl.when(jnp.logical_and(live, fresh))
    def _():
        wgu_b[...] = wgu_ref[0].astype(BF16)
        wd_b[...] = wd_ref[0].astype(BF16)

    @pl.when(live)
    def _():
        x16 = _from_tiles(xs_ref, 0, tm).astype(BF16)
        y = None
        for c in range(D_FF // FF_CHUNK):
            lo, hi = c * FF_CHUNK, (c + 1) * FF_CHUNK
            hg = jnp.dot(x16, wgu_b[:, lo:hi], preferred_element_type=F32) + bgu_ref[0, :, lo:hi]
            hu = (jnp.dot(x16, wgu_b[:, D_FF + lo:D_FF + hi], preferred_element_type=F32)
                  + bgu_ref[0, :, D_FF + lo:D_FF + hi])
            gate = jnp.minimum(hg, SWIGLU_LIMIT)
            up = jnp.clip(hu, -SWIGLU_LIMIT, SWIGLU_LIMIT)
            act = ((up + 1.0) * (gate * _sigmoid(SWIGLU_ALPHA * gate))).astype(BF16)
            part = jnp.dot(act, wd_b[lo:hi, :], preferred_element_type=F32)
            y = part if y is None else y + part
        _to_tiles(ys_ref, y + bd_ref[0])

    @pl.when(jnp.logical_not(live))
    def _():
        ys_ref[...] = jnp.zeros(ys_ref.shape, F32)


def _experts(xs, block_e, nused, wgu, bgu, wd, bd, layer, tm):
    n_rows = xs.shape[0] // SUBL
    nb = n_rows // tm

    def rows(i, be, nu):
        return (jnp.minimum(i, nu[0] - 1), 0)

    def per_e(i, be, nu):
        return (layer * N_EXPERTS + be[i], 0, 0)

    grid_spec = pltpu.PrefetchScalarGridSpec(
        num_scalar_prefetch=2,
        grid=(nb,),
        in_specs=[pl.BlockSpec((tm * SUBL, LANES), rows),
                  pl.BlockSpec((1, D_MODEL, 2 * D_FF), per_e),
                  pl.BlockSpec((1, 1, 2 * D_FF), per_e),
                  pl.BlockSpec((1, D_FF, D_MODEL), per_e),
                  pl.BlockSpec((1, 1, D_MODEL), per_e)],
        out_specs=pl.BlockSpec((tm * SUBL, LANES), lambda i, be, nu: (i, 0)),
        scratch_shapes=[pltpu.VMEM((D_MODEL, 2 * D_FF), BF16), pltpu.VMEM((D_FF, D_MODEL), BF16)],
    )
    return pl.pallas_call(
        functools.partial(_expert_kernel, tm=tm),
        grid_spec=grid_spec,
        out_shape=jax.ShapeDtypeStruct((n_rows * SUBL, LANES), F32),
        compiler_params=_cparams(("arbitrary",)),
        name="moe_experts",
    )(block_e, nused, xs, wgu, bgu, wd, bd)


def _combine_kernel(dcur_ref, dnext_ref, ys_ref, x1_ref, tg_ref, g_ref, b_ref, o_ref, buf, sem, *, tm):
    i = pl.program_id(0)
    last = pl.num_programs(0) - 1
    p = i % 2
    rows = TOP_K * tm * SUBL

    def fetch(dest_ref, slot):
        def issue(t, carry):
            for k in range(TOP_K):
                pltpu.make_async_copy(_tile(ys_ref, dest_ref[t * TOP_K + k]),
                                      _tile(buf.at[slot], k * tm + t), sem.at[slot]).start(priority=k % 2)
            return carry

        lax.fori_loop(0, tm, issue, 0)

    @pl.when(i == 0)
    def _():
        fetch(dcur_ref, 0)

    @pl.when(i < last)
    def _():
        fetch(dnext_ref, 1 - p)

    pltpu.make_async_copy(ys_ref.at[pl.ds(0, rows), :], buf.at[p], sem.at[p]).wait()
    tg = tg_ref[...]
    cur = buf.at[p]
    y = tg[:, 0:1] * _from_tiles(cur, 0, tm)
    for k in range(1, TOP_K):
        y = y + tg[:, k:k + 1] * _from_tiles(cur, k * tm, tm)
    o_ref[...] = _layer_norm(DN_ALPHA * x1_ref[...] + y, g_ref[...], b_ref[...])


def _combine(ys, dest_flat, x1, tg, ln_g, ln_b, tm):
    n = x1.shape[0]
    nsteps = n // tm
    kern = functools.partial(_combine_kernel, tm=tm)
    return pl.pallas_call(
        kern,
        grid=(nsteps,),
        in_specs=[pl.BlockSpec((tm * TOP_K,), lambda i: (i,), memory_space=pltpu.SMEM),
                  pl.BlockSpec((tm * TOP_K,), lambda i: (jnp.minimum(i + 1, nsteps - 1),),
                               memory_space=pltpu.SMEM),
                  pl.BlockSpec(memory_space=pl.ANY),
                  pl.BlockSpec((tm, D_MODEL), lambda i: (i, 0)),
                  pl.BlockSpec((tm, LANES), lambda i: (i, 0)),
                  pl.BlockSpec((1, D_MODEL), lambda i: (0, 0)),
                  pl.BlockSpec((1, D_MODEL), lambda i: (0, 0))],
        out_specs=pl.BlockSpec((tm, D_MODEL), lambda i: (i, 0)),
        out_shape=jax.ShapeDtypeStruct((n, D_MODEL), F32),
        scratch_shapes=[pltpu.VMEM((2, TOP_K * tm * SUBL, LANES), F32), pltpu.SemaphoreType.DMA((2,))],
        compiler_params=_cparams(("arbitrary",)),
        name="moe_combine",
    )(dest_flat, dest_flat, ys, x1, tg, ln_g, ln_b)


def _pad_lanes(v, width=LANES, value=0.0):
    return jnp.pad(v, ((0, 0), (0, width - v.shape[-1])), constant_values=value)


def _prep_w_in(w):
    body = w[:, :6 * SEC]
    gi = _pad_lanes(w[:, 6 * SEC:6 * SEC + ML_HEADS])
    gf = _pad_lanes(w[:, 6 * SEC + ML_HEADS:6 * SEC + 2 * ML_HEADS])
    return jnp.concatenate([body, gi, gf], axis=1).astype(BF16)


def _layer(x, p, big, layer, lam_init, cfg):
    n = x.shape[0]
    b, s = cfg["batch"], cfg["seq"]
    tm_e = cfg["tm_expert"]

    qkv, ml, gates = _inproj(x, _prep_w_in(p["w_in"]), cfg["tm_proj"])

    lamv = jnp.zeros((8, LANES), F32)
    for r, nm in enumerate(("lam_q1", "lam_k1", "lam_q2", "lam_k2")):
        lamv = lamv.at[r, :DA_QK_DIM].set(p[nm])
    da = _diff_attention(qkv.reshape(b, s, 3 * SEC), lamv, p["da_norm_g"].reshape(1, LANES),
                         lam_init, cfg["tq"], cfg["tk"])

    gate_b = jnp.concatenate([_pad_lanes(p["gate_b"][None, :ML_HEADS]),
                              _pad_lanes(p["gate_b"][None, ML_HEADS:])], axis=1)
    mlo = _mlstm(ml.reshape(b, s, 3 * SEC), gates.reshape(b, s, GATE_W), p["conv_w"],
                 p["conv_b"].reshape(1, SEC), gate_b, cfg["chunk"], cfg["ml_group"])

    wr = _pad_lanes(p["w_router"])
    wr_hi = wr.astype(BF16)
    wr = jnp.concatenate([wr_hi, (wr - wr_hi.astype(F32)).astype(BF16)], axis=1)
    br = _pad_lanes(p["b_router"][None, :], value=NEG_INF)
    x1, x1t, te, tg, rk, cnt = _outproj_router(
        da.reshape(n, SEC), mlo.reshape(n, SEC), x, p["w_out"].astype(BF16),
        p["ln1_g"].reshape(1, D_MODEL), p["ln1_b"].reshape(1, D_MODEL), wr, br, cfg["tm_proj"])

    counts = cnt[0, :N_EXPERTS]
    padded = (counts + tm_e - 1) // tm_e * tm_e
    pad_ends = jnp.cumsum(padded)
    pad_starts = pad_ends - padded
    top_e = te[:, :TOP_K]
    sel = top_e[:, :, None] == jnp.arange(N_EXPERTS, dtype=I32)[None, None, :]
    dest = rk[:, :TOP_K] + jnp.sum(jnp.where(sel, pad_starts[None, None, :], 0), axis=-1)
    dest_flat = dest.reshape(n * TOP_K).astype(I32)
    n_rows = n * TOP_K + N_EXPERTS * tm_e
    nb = n_rows // tm_e
    blk_start = jnp.arange(nb, dtype=I32) * tm_e
    block_e = jnp.minimum(jnp.sum(blk_start[:, None] >= pad_ends[None, :], axis=-1),
                          N_EXPERTS - 1).astype(I32)
    nused = (pad_ends[-1:] // tm_e).astype(I32)
    seg_lo = (pad_starts + counts).astype(I32)
    seg_hi = pad_ends.at[N_EXPERTS - 1].set(n_rows).astype(I32)

    xs = _dispatch(x1t, dest_flat, seg_lo, seg_hi, n_rows, cfg["tm_disp"])
    ys = _experts(xs, block_e, nused, big["w_gu"], big["b_gu"], big["w_down"], big["b_down"],
                  layer, tm_e)
    return _combine(ys, dest_flat, x1, tg, p["ln2_g"].reshape(1, D_MODEL),
                    p["ln2_b"].reshape(1, D_MODEL), cfg["tm_comb"])


def _forward(x, params, cfg):
    b, s, d = x.shape
    h = x.reshape(b * s, d)
    depth = params["w_in"].shape[0]
    big = dict(w_gu=params["w_gu"].reshape(depth * N_EXPERTS, D_MODEL, 2 * D_FF),
               b_gu=params["b_gu"].reshape(depth * N_EXPERTS, 1, 2 * D_FF),
               w_down=params["w_down"].reshape(depth * N_EXPERTS, D_FF, D_MODEL),
               b_down=params["b_down"].reshape(depth * N_EXPERTS, 1, D_MODEL))
    for l in range(depth):
        lam_init = 0.8 - 0.6 * math.exp(-0.3 * l)
        p = {k: v[l] for k, v in params.items() if k not in big}
        h = _layer(h, p, big, l, lam_init, cfg)
    return h.reshape(b, s, d)


def kernel(x, w_in, conv_w, conv_b, gate_b, lam_q1, lam_k1, lam_q2, lam_k2, da_norm_g, w_out,
           ln1_g, ln1_b, w_router, b_router, w_gu, b_gu, w_down, b_down, ln2_g, ln2_b):
    params = dict(w_in=w_in, conv_w=conv_w, conv_b=conv_b, gate_b=gate_b, lam_q1=lam_q1,
                  lam_k1=lam_k1, lam_q2=lam_q2, lam_k2=lam_k2, da_norm_g=da_norm_g, w_out=w_out,
                  ln1_g=ln1_g, ln1_b=ln1_b, w_router=w_router, b_router=b_router, w_gu=w_gu,
                  b_gu=b_gu, w_down=w_down, b_down=b_down, ln2_g=ln2_g, ln2_b=ln2_b)
    cfg = dict(batch=x.shape[0], seq=x.shape[1], tm_proj=512, tq=2048, tk=512, chunk=256, ml_group=2,
               tm_expert=512, tm_disp=256, tm_comb=256)
    return _forward(x, params, cfg)
```

```python
import functools
import math

import jax
import jax.numpy as jnp
from jax import lax
from jax.experimental import pallas as pl
from jax.experimental.pallas import tpu as pltpu

F32 = jnp.float32
BF16 = jnp.bfloat16
I32 = jnp.int32

D_MODEL = 1024
DEPTH = 4
DA_HEADS = 4
DA_QK_DIM = 64
DA_V_DIM = 128
ML_HEADS = 4
ML_QK_DIM = 64
ML_V_DIM = 128
CONV_WIDTH = 4
N_EXPERTS = 32
TOP_K = 4
D_FF = 1024
SWIGLU_LIMIT = 7.0
SWIGLU_ALPHA = 1.702
DN_ALPHA = (2 * DEPTH) ** 0.25
LN_EPS = 1e-5
RMS_EPS = 1e-5

LANES = 128
SEC = 512
GATE_W = 2 * LANES
IN_W_PAD = 6 * SEC + GATE_W
VMEM_LIMIT = 56 * 1024 * 1024
NEG_INF = float("-inf")


def _cparams(sem):
    return pltpu.CompilerParams(dimension_semantics=sem, vmem_limit_bytes=VMEM_LIMIT)


def _inproj_kernel(x_ref, w_ref, qkv_ref, ml_ref, gate_ref):
    xb = x_ref[...].astype(BF16)
    for c in range(3):
        sl = slice(c * SEC, (c + 1) * SEC)
        qkv_ref[:, sl] = jnp.dot(xb, w_ref[:, sl], preferred_element_type=F32).astype(BF16)
    for c in range(3):
        ml_ref[:, c * SEC:(c + 1) * SEC] = jnp.dot(
            xb, w_ref[:, (3 + c) * SEC:(4 + c) * SEC], preferred_element_type=F32)
    gate_ref[...] = jnp.dot(xb, w_ref[:, 6 * SEC:], preferred_element_type=F32)


def _inproj(x, w, tm):
    n = x.shape[0]
    return pl.pallas_call(
        _inproj_kernel,
        grid=(n // tm,),
        in_specs=[pl.BlockSpec((tm, D_MODEL), lambda i: (i, 0)),
                  pl.BlockSpec((D_MODEL, IN_W_PAD), lambda i: (0, 0))],
        out_specs=[pl.BlockSpec((tm, 3 * SEC), lambda i: (i, 0)),
                   pl.BlockSpec((tm, 3 * SEC), lambda i: (i, 0)),
                   pl.BlockSpec((tm, GATE_W), lambda i: (i, 0))],
        out_shape=[jax.ShapeDtypeStruct((n, 3 * SEC), BF16),
                   jax.ShapeDtypeStruct((n, 3 * SEC), F32),
                   jax.ShapeDtypeStruct((n, GATE_W), F32)],
        compiler_params=_cparams(("parallel",)),
        name="in_proj",
    )(x, w)


def _attn_kernel(qi_tab, ki_tab, q_ref, k_ref, v_ref, lamv_ref, g_ref, o_ref, qs, m_s, acc,
                 *, lam_init, tq, tk):
    step = pl.program_id(2)
    qi = qi_tab[step]
    ki = ki_tab[step]
    ng = tq // tk
    reps = tk // LANES

    @pl.when(ki == 0)
    def _():
        q = q_ref[0] * jnp.asarray(DA_QK_DIM ** -0.5, BF16)
        lane = lax.broadcasted_iota(I32, q.shape, 1)
        zero = jnp.zeros_like(q)
        qs[0:tq, :] = jnp.where(lane < DA_QK_DIM, q, zero)
        qs[tq:2 * tq, :] = jnp.where(lane >= DA_QK_DIM, q, zero)
        m_s[...] = jnp.full(m_s.shape, NEG_INF, F32)
        acc[...] = jnp.zeros(acc.shape, F32)

    def update(diag):
        k = k_ref[0]
        v_aug = jnp.concatenate([v_ref[0], jnp.ones((tk, LANES), BF16)], axis=1)
        groups = [(gi, slice(mp * tq + gi * tk, mp * tq + (gi + 1) * tk))
                  for gi in range(ng) if gi >= diag for mp in range(2)]
        ss = [lax.dot_general(qs[g, :], k, (((1,), (1,)), ((), ())), preferred_element_type=F32)
              for _, g in groups]
        if diag >= 0:
            row = lax.broadcasted_iota(I32, (tk, tk), 0)
            col = lax.broadcasted_iota(I32, (tk, tk), 1)
            ss = [jnp.where(row >= col, s, NEG_INF) if gi == diag else s
                  for (gi, _), s in zip(groups, ss)]
        for (_, g), s in zip(groups, ss):
            m_old = m_s[g, :]
            m_new = jnp.maximum(m_old, jnp.max(s, axis=-1, keepdims=True))
            p = jnp.exp(s - jnp.tile(m_new, (1, reps)))
            alpha = jnp.exp(m_old - m_new)
            pv = jnp.dot(p.astype(BF16), v_aug, preferred_element_type=F32)
            acc[g, :] = jnp.tile(alpha, (1, 2)) * acc[g, :] + pv
            m_s[g, :] = m_new

    @pl.when(ki < qi * ng)
    def _():
        update(-1)

    for d in range(ng):
        @pl.when(ki == qi * ng + d)
        def _(d=d):
            update(d)

    @pl.when(ki == qi * ng + ng - 1)
    def _():
        lamv = lamv_ref[...]
        lam = (jnp.exp(jnp.sum(lamv[0:1] * lamv[1:2], axis=-1, keepdims=True))
               - jnp.exp(jnp.sum(lamv[2:3] * lamv[3:4], axis=-1, keepdims=True)) + lam_init)
        on = acc[:, 0:LANES] / acc[:, LANES:2 * LANES]
        o = on[0:tq] - lam * on[tq:2 * tq]
        ms = jnp.mean(o * o, axis=-1, keepdims=True)
        o = o * lax.rsqrt(ms + RMS_EPS) * g_ref[...] * (1.0 - lam_init)
        o_ref[0] = o.astype(o_ref.dtype)


def _diff_attention(qkv, lamv, norm_g, lam_init, tq, tk):
    b, s, _ = qkv.shape
    ng = tq // tk
    pairs = [(qi, ki) for qi in range(s // tq) for ki in range((qi + 1) * ng)]
    qi_tab = jnp.asarray([p[0] for p in pairs], I32)
    ki_tab = jnp.asarray([p[1] for p in pairs], I32)
    kern = functools.partial(_attn_kernel, lam_init=lam_init, tq=tq, tk=tk)
    grid_spec = pltpu.PrefetchScalarGridSpec(
        num_scalar_prefetch=2,
        grid=(b, DA_HEADS, len(pairs)),
        in_specs=[
            pl.BlockSpec((1, tq, LANES), lambda bi, h, t, qt, kt: (bi, qt[t], h)),
            pl.BlockSpec((1, tk, LANES), lambda bi, h, t, qt, kt: (bi, kt[t], DA_HEADS + h)),
            pl.BlockSpec((1, tk, LANES), lambda bi, h, t, qt, kt: (bi, kt[t], 2 * DA_HEADS + h)),
            pl.BlockSpec((8, LANES), lambda bi, h, t, qt, kt: (0, 0)),
            pl.BlockSpec((1, LANES), lambda bi, h, t, qt, kt: (0, 0)),
        ],
        out_specs=pl.BlockSpec((1, tq, LANES), lambda bi, h, t, qt, kt: (bi, qt[t], h)),
        scratch_shapes=[pltpu.VMEM((2 * tq, LANES), BF16), pltpu.VMEM((2 * tq, LANES), F32),
                        pltpu.VMEM((2 * tq, 2 * LANES), F32)],
    )
    return pl.pallas_call(
        kern,
        grid_spec=grid_spec,
        out_shape=jax.ShapeDtypeStruct((b, s, SEC), BF16),
        compiler_params=_cparams(("parallel", "parallel", "arbitrary")),
        name="diff_attn",
    )(qi_tab, ki_tab, qkv, qkv, qkv, lamv, norm_g)


def _log_sigmoid(x):
    return jnp.minimum(x, 0.0) - jnp.log1p(jnp.exp(-jnp.abs(x)))


def _sigmoid(x):
    return 1.0 / (1.0 + jnp.exp(-x))


def _mlstm_kernel(ml_ref, gt_ref, cw_ref, cb_ref, gb_ref, o_ref, xbuf, cst, mst, *, chunk, group):
    @pl.when(pl.program_id(1) == 0)
    def _():
        xbuf[:, 0:ML_HALO, :] = jnp.zeros((group, ML_HALO, SEC), F32)
        cst[...] = jnp.zeros(cst.shape, F32)
        mst[...] = jnp.full(mst.shape, NEG_INF, F32)

    for bb in range(group):
        _mlstm_chunk(bb, ml_ref, gt_ref, cw_ref, cb_ref, gb_ref, o_ref, xbuf, cst, mst, chunk)


ML_HALO = 8


def _mlstm_chunk(bb, ml_ref, gt_ref, cw_ref, cb_ref, gb_ref, o_ref, xbuf, cst, mst, chunk):
    L = chunk
    HALO = ML_HALO

    xbuf[bb, HALO:HALO + L, :] = ml_ref[bb, :, 0:SEC]
    y = cb_ref[...] + cw_ref[CONV_WIDTH - 1:CONV_WIDTH, :] * xbuf[bb, HALO:HALO + L, :]
    for j in range(CONV_WIDTH - 1):
        off = HALO - (CONV_WIDTH - 1) + j
        y = y + cw_ref[j:j + 1, :] * xbuf[bb, off:off + L, :]
    xbuf[bb, 0:HALO, :] = xbuf[bb, L:L + HALO, :]
    qk = y * _sigmoid(y)

    gi = gt_ref[bb, :, 0:LANES] + gb_ref[:, 0:LANES]
    lf = _log_sigmoid(gt_ref[bb, :, LANES:GATE_W] + gb_ref[:, LANES:GATE_W])
    row = lax.broadcasted_iota(I32, (L, L), 0)
    col = lax.broadcasted_iota(I32, (L, L), 1)
    causal = row >= col
    bcum = jnp.dot(causal.astype(F32), lf, preferred_element_type=F32,
                   precision=lax.Precision.HIGHEST)
    g = bcum[L - 1:L, :]
    w_end = g - bcum + gi
    m_loc = jnp.max(w_end, axis=0, keepdims=True)
    e_end = jnp.exp(w_end - m_loc)
    m_prev = mst[bb]
    m_new = jnp.maximum(g + m_prev, m_loc)
    a_dec = jnp.exp(g + m_prev - m_new)
    b_dec = jnp.exp(m_loc - m_new)
    mst[bb] = m_new
    inter_log = bcum + m_prev
    bcum_t = bcum.T
    gi_t = gi.T

    lane = lax.broadcasted_iota(I32, (L, LANES), 1)
    sub = lax.broadcasted_iota(I32, (LANES, 1), 0)
    ones_col = jnp.where(lane == 0, 1.0, 0.0).astype(BF16)
    for p in range(ML_HEADS // 2):
        q_pair = qk[:, p * LANES:(p + 1) * LANES] * (ML_QK_DIM ** -0.5)
        k_pair = qk[:, 2 * LANES + p * LANES:2 * LANES + (p + 1) * LANES]
        kb = k_pair.astype(BF16)
        cp = bb * (ML_HEADS // 2) + p
        c_prev = cst[cp].astype(BF16)
        upd = jnp.zeros((LANES, 2 * LANES), F32)
        for hh in range(2):
            h = 2 * p + hh
            head_lanes = (lane >= hh * ML_QK_DIM) & (lane < (hh + 1) * ML_QK_DIM)
            qm = jnp.where(head_lanes, q_pair, 0.0).astype(BF16)
            s = lax.dot_general(qm, kb, (((1,), (1,)), ((), ())), preferred_element_type=F32)
            d = jnp.where(causal, bcum[:, h:h + 1] - bcum_t[h:h + 1, :] + gi_t[h:h + 1, :], NEG_INF)
            il = inter_log[:, h:h + 1]
            m_t = jnp.maximum(il, jnp.max(d, axis=-1, keepdims=True))
            sd = (s * jnp.exp(d - m_t)).astype(BF16)
            inter_w = jnp.exp(il - m_t)
            v_aug = jnp.concatenate(
                [ml_ref[bb, :, SEC + h * LANES:SEC + (h + 1) * LANES].astype(BF16), ones_col], axis=1)
            intra = jnp.dot(sd, v_aug, preferred_element_type=F32)
            inter = jnp.dot(qm, c_prev, preferred_element_type=F32)
            num = inter_w * inter[:, 0:LANES] + intra[:, 0:LANES]
            den = inter_w * inter[:, LANES:LANES + 1] + intra[:, LANES:LANES + 1]
            hid = num / jnp.maximum(jnp.abs(den), jnp.exp(-m_t))
            o_gate = ml_ref[bb, :, 2 * SEC + h * LANES:2 * SEC + (h + 1) * LANES]
            o_ref[bb, :, h * LANES:(h + 1) * LANES] = (hid * _sigmoid(o_gate)).astype(o_ref.dtype)
            ek = jnp.where(head_lanes, e_end[:, h:h + 1] * k_pair, 0.0).astype(BF16)
            upd = upd + lax.dot_general(ek, v_aug, (((0,), (0,)), ((), ())),
                                        preferred_element_type=F32)
        first = sub < ML_QK_DIM
        a_rows = jnp.where(first, a_dec[:, 2 * p:2 * p + 1], a_dec[:, 2 * p + 1:2 * p + 2])
        b_rows = jnp.where(first, b_dec[:, 2 * p:2 * p + 1], b_dec[:, 2 * p + 1:2 * p + 2])
        cst[cp] = a_rows * cst[cp] + b_rows * upd


def _mlstm(ml, gates, conv_w, conv_b, gate_b, chunk, group):
    b, s, _ = ml.shape
    kern = functools.partial(_mlstm_kernel, chunk=chunk, group=group)
    return pl.pallas_call(
        kern,
        grid=(b // group, s // chunk),
        in_specs=[pl.BlockSpec((group, chunk, 3 * SEC), lambda bi, c: (bi, c, 0)),
                  pl.BlockSpec((group, chunk, GATE_W), lambda bi, c: (bi, c, 0)),
                  pl.BlockSpec((CONV_WIDTH, SEC), lambda bi, c: (0, 0)),
                  pl.BlockSpec((1, SEC), lambda bi, c: (0, 0)),
                  pl.BlockSpec((1, GATE_W), lambda bi, c: (0, 0))],
        out_specs=pl.BlockSpec((group, chunk, SEC), lambda bi, c: (bi, c, 0)),
        out_shape=jax.ShapeDtypeStruct((b, s, SEC), BF16),
        scratch_shapes=[pltpu.VMEM((group, chunk + 8, SEC), F32),
                        pltpu.VMEM((group * (ML_HEADS // 2), LANES, 2 * LANES), F32),
                        pltpu.VMEM((group, 1, LANES), F32)],
        compiler_params=_cparams(("parallel", "arbitrary")),
        name="mlstm",
    )(ml, gates, conv_w, conv_b, gate_b)


SUBL = D_MODEL // LANES


def _to_tiles(ref, val, base=0):
    rows = val.shape[0]
    for s in range(SUBL):
        ref[pl.ds(base * SUBL + s, rows, stride=SUBL), :] = val[:, s * LANES:(s + 1) * LANES]


def _tile(ref, t):
    return ref.at[pl.ds(pl.multiple_of(t * SUBL, SUBL), SUBL), :]


def _from_tiles(ref, base, rows):
    return jnp.concatenate(
        [ref[pl.ds(base * SUBL + s, rows, stride=SUBL), :] for s in range(SUBL)], axis=1)


def _layer_norm(h, g, b):
    mu = jnp.mean(h, axis=-1, keepdims=True)
    hc = h - mu
    var = jnp.mean(hc * hc, axis=-1, keepdims=True)
    return hc * lax.rsqrt(var + LN_EPS) * g + b


def _outproj_router_kernel(da_ref, mlo_ref, x_ref, wo_ref, g_ref, b_ref, wr_ref, br_ref,
                           x1_ref, x1t_ref, te_ref, tg_ref, rk_ref, cnt_ref, carry, *, tm):
    i = pl.program_id(0)

    @pl.when(i == 0)
    def _():
        carry[...] = jnp.zeros(carry.shape, F32)

    mix = (jnp.dot(da_ref[...], wo_ref[0:SEC, :], preferred_element_type=F32)
           + jnp.dot(mlo_ref[...], wo_ref[SEC:2 * SEC, :], preferred_element_type=F32))
    x1 = _layer_norm(DN_ALPHA * x_ref[...] + mix, g_ref[...], b_ref[...])
    x1_ref[...] = x1
    _to_tiles(x1t_ref, x1)

    x_hi = x1.astype(BF16)
    x_lo = (x1 - x_hi.astype(F32)).astype(BF16)
    hh_hl = jnp.dot(x_hi, wr_ref[...], preferred_element_type=F32)
    lh = jnp.dot(x_lo, wr_ref[:, 0:LANES], preferred_element_type=F32)
    logits = hh_hl[:, 0:LANES] + (hh_hl[:, LANES:2 * LANES] + lh) + br_ref[...]
    lane = lax.broadcasted_iota(I32, (tm, LANES), 1)
    lane_f = lane.astype(F32)
    work = logits
    hot = []
    vals = []
    idxs = []
    for _k in range(TOP_K):
        mx = jnp.max(work, axis=-1, keepdims=True)
        idx = jnp.min(jnp.where(work == mx, lane_f, float(LANES)), axis=-1, keepdims=True)
        sel = lane_f == idx
        hot.append(sel)
        vals.append(mx)
        idxs.append(idx)
        work = jnp.where(sel, NEG_INF, work)
    exps = [jnp.exp(v - vals[0]) for v in vals]
    inv = 1.0 / (exps[0] + exps[1] + exps[2] + exps[3])

    onehot = (hot[0] | hot[1] | hot[2] | hot[3])
    oh = jnp.where(onehot, 1.0, 0.0)
    row = lax.broadcasted_iota(I32, (tm, tm), 0)
    col = lax.broadcasted_iota(I32, (tm, tm), 1)
    before = jnp.dot((row > col).astype(BF16), oh.astype(BF16), preferred_element_type=F32)
    pos = before + carry[...]
    te = jnp.zeros((tm, LANES), I32)
    tg = jnp.zeros((tm, LANES), F32)
    rk = jnp.zeros((tm, LANES), I32)
    for k in range(TOP_K):
        slot = lane == k
        r_k = jnp.sum(jnp.where(hot[k], pos, 0.0), axis=-1, keepdims=True)
        te = jnp.where(slot, idxs[k].astype(I32), te)
        tg = jnp.where(slot, exps[k] * inv, tg)
        rk = jnp.where(slot, r_k.astype(I32), rk)
    te_ref[...] = te
    tg_ref[...] = tg
    rk_ref[...] = rk
    carry[...] = carry[...] + jnp.sum(oh, axis=0, keepdims=True)
    cnt_ref[...] = jnp.broadcast_to(carry[...], cnt_ref.shape).astype(I32)


def _outproj_router(da, mlo, x, wo, ln_g, ln_b, wr, br, tm):
    n = x.shape[0]
    kern = functools.partial(_outproj_router_kernel, tm=tm)
    row_blk = lambda w: pl.BlockSpec((tm, w), lambda i: (i, 0))
    const = lambda r, w: pl.BlockSpec((r, w), lambda i: (0, 0))
    return pl.pallas_call(
        kern,
        grid=(n // tm,),
        in_specs=[row_blk(SEC), row_blk(SEC), row_blk(D_MODEL), const(2 * SEC, D_MODEL),
                  const(1, D_MODEL), const(1, D_MODEL), const(D_MODEL, 2 * LANES), const(1, LANES)],
        out_specs=[row_blk(D_MODEL), pl.BlockSpec((tm * SUBL, LANES), lambda i: (i, 0)),
                   row_blk(LANES), row_blk(LANES), row_blk(LANES), const(8, LANES)],
        out_shape=[jax.ShapeDtypeStruct((n, D_MODEL), F32),
                   jax.ShapeDtypeStruct((n * SUBL, LANES), F32),
                   jax.ShapeDtypeStruct((n, LANES), I32),
                   jax.ShapeDtypeStruct((n, LANES), F32),
                   jax.ShapeDtypeStruct((n, LANES), I32),
                   jax.ShapeDtypeStruct((8, LANES), I32)],
        scratch_shapes=[pltpu.VMEM((1, LANES), F32)],
        compiler_params=_cparams(("arbitrary",)),
        name="outproj_router",
    )(da, mlo, x, wo, ln_g, ln_b, wr, br)


def _dispatch_kernel(dest_ref, lo_ref, hi_ref, x_ref, xs_ref, zeros, sem, zsem, *, tm):
    i = pl.program_id(0)

    def issue(t, carry):
        for k in range(TOP_K):
            pltpu.make_async_copy(_tile(x_ref, t), _tile(xs_ref, dest_ref[t * TOP_K + k]),
                                  sem).start(priority=k % 2)
        return carry

    lax.fori_loop(0, tm, issue, 0)

    @pl.when(i == 0)
    def _():
        zeros[...] = jnp.zeros(zeros.shape, F32)

        def per_expert(e, carry):
            def fill(r, c):
                pltpu.make_async_copy(zeros, _tile(xs_ref, r), zsem).start()
                return c

            def drain(r, c):
                pltpu.make_async_copy(zeros, _tile(xs_ref, r), zsem).wait()
                return c

            lax.fori_loop(lo_ref[e], hi_ref[e], fill, 0)
            lax.fori_loop(lo_ref[e], hi_ref[e], drain, 0)
            return carry

        lax.fori_loop(0, N_EXPERTS, per_expert, 0)

    for k in range(TOP_K):
        pltpu.make_async_copy(x_ref, xs_ref.at[pl.ds(0, tm * SUBL), :], sem).wait()


def _dispatch(x1t, dest_flat, seg_lo, seg_hi, n_rows, tm):
    n = x1t.shape[0] // SUBL
    kern = functools.partial(_dispatch_kernel, tm=tm)
    return pl.pallas_call(
        kern,
        grid=(n // tm,),
        in_specs=[pl.BlockSpec((tm * TOP_K,), lambda i: (i,), memory_space=pltpu.SMEM),
                  pl.BlockSpec(memory_space=pltpu.SMEM),
                  pl.BlockSpec(memory_space=pltpu.SMEM),
                  pl.BlockSpec((tm * SUBL, LANES), lambda i: (i, 0))],
        out_specs=pl.BlockSpec(memory_space=pl.ANY),
        out_shape=jax.ShapeDtypeStruct((n_rows * SUBL, LANES), F32),
        scratch_shapes=[pltpu.VMEM((SUBL, LANES), F32), pltpu.SemaphoreType.DMA(()),
                        pltpu.SemaphoreType.DMA(())],
        compiler_params=_cparams(("arbitrary",)),
        name="moe_dispatch",
    )(dest_flat, seg_lo, seg_hi, x1t)


FF_CHUNK = 256


def _expert_kernel(be_ref, nused_ref, nxt_ref, xs_ref, wgu_hbm, bgu_ref, wd_hbm, bd_ref, ys_ref,
                   wgu_b, wd_b, wgu_f, wd_f, wsem, *, tm, layer):
    i = pl.program_id(0)
    live = i < nused_ref[0]
    fresh = jnp.logical_or(i == 0, be_ref[i] != be_ref[jnp.maximum(i - 1, 0)])

    def weight_copies(e):
        w = layer * N_EXPERTS + e
        return (pltpu.make_async_copy(wgu_hbm.at[w], wgu_f, wsem.at[0]),
                pltpu.make_async_copy(wd_hbm.at[w], wd_f, wsem.at[1]))

    @pl.when(i == 0)
    def _():
        for cp in weight_copies(be_ref[0]):
            cp.start()

    @pl.when(jnp.logical_and(live, fresh))
    def _():
        for cp in weight_copies(be_ref[i]):
            cp.wait()
        wgu_b[...] = wgu_f[...].astype(BF16)
        wd_b[...] = wd_f[...].astype(BF16)

        @pl.when(nxt_ref[i] >= 0)
        def _():
            for cp in weight_copies(nxt_ref[i]):
                cp.start()

    @pl.when(live)
    def _():
        x16 = _from_tiles(xs_ref, 0, tm).astype(BF16)
        y = None
        for c in range(D_FF // FF_CHUNK):
            lo, hi = c * FF_CHUNK, (c + 1) * FF_CHUNK
            hg = jnp.dot(x16, wgu_b[:, lo:hi], preferred_element_type=F32) + bgu_ref[0, :, lo:hi]
            hu = (jnp.dot(x16, wgu_b[:, D_FF + lo:D_FF + hi], preferred_element_type=F32)
                  + bgu_ref[0, :, D_FF + lo:D_FF + hi])
            gate = jnp.minimum(hg, SWIGLU_LIMIT)
            up = jnp.clip(hu, -SWIGLU_LIMIT, SWIGLU_LIMIT)
            act = ((up + 1.0) * (gate * _sigmoid(SWIGLU_ALPHA * gate))).astype(BF16)
            part = jnp.dot(act, wd_b[lo:hi, :], preferred_element_type=F32)
            y = part if y is None else y + part
        _to_tiles(ys_ref, y + bd_ref[0])

    @pl.when(jnp.logical_not(live))
    def _():
        ys_ref[...] = jnp.zeros(ys_ref.shape, F32)


def _experts(xs, block_e, nused, next_e, wgu, bgu, wd, bd, layer, tm):
    n_rows = xs.shape[0] // SUBL
    nb = n_rows // tm

    def rows(i, be, nu, nx):
        return (jnp.minimum(i, nu[0] - 1), 0)

    def per_e(i, be, nu, nx):
        return (layer * N_EXPERTS + be[i], 0, 0)

    grid_spec = pltpu.PrefetchScalarGridSpec(
        num_scalar_prefetch=3,
        grid=(nb,),
        in_specs=[pl.BlockSpec((tm * SUBL, LANES), rows),
                  pl.BlockSpec(memory_space=pl.ANY),
                  pl.BlockSpec((1, 1, 2 * D_FF), per_e),
                  pl.BlockSpec(memory_space=pl.ANY),
                  pl.BlockSpec((1, 1, D_MODEL), per_e)],
        out_specs=pl.BlockSpec((tm * SUBL, LANES), lambda i, be, nu, nx: (i, 0)),
        scratch_shapes=[pltpu.VMEM((D_MODEL, 2 * D_FF), BF16), pltpu.VMEM((D_FF, D_MODEL), BF16),
                        pltpu.VMEM((D_MODEL, 2 * D_FF), F32), pltpu.VMEM((D_FF, D_MODEL), F32),
                        pltpu.SemaphoreType.DMA((2,))],
    )
    return pl.pallas_call(
        functools.partial(_expert_kernel, tm=tm, layer=layer),
        grid_spec=grid_spec,
        out_shape=jax.ShapeDtypeStruct((n_rows * SUBL, LANES), F32),
        compiler_params=_cparams(("arbitrary",)),
        name="moe_experts",
    )(block_e, nused, next_e, xs, wgu, bgu, wd, bd)


def _combine_kernel(dcur_ref, dnext_ref, ys_ref, x1_ref, tg_ref, g_ref, b_ref, o_ref, buf, sem, *, tm):
    i = pl.program_id(0)
    last = pl.num_programs(0) - 1
    p = i % 2
    rows = TOP_K * tm * SUBL

    def fetch(dest_ref, slot):
        def issue(t, carry):
            for k in range(TOP_K):
                pltpu.make_async_copy(_tile(ys_ref, dest_ref[t * TOP_K + k]),
                                      _tile(buf.at[slot], k * tm + t), sem.at[slot]).start(priority=k % 2)
            return carry

        lax.fori_loop(0, tm, issue, 0)

    @pl.when(i == 0)
    def _():
        fetch(dcur_ref, 0)

    @pl.when(i < last)
    def _():
        fetch(dnext_ref, 1 - p)

    pltpu.make_async_copy(ys_ref.at[pl.ds(0, rows), :], buf.at[p], sem.at[p]).wait()
    tg = tg_ref[...]
    cur = buf.at[p]
    y = tg[:, 0:1] * _from_tiles(cur, 0, tm)
    for k in range(1, TOP_K):
        y = y + tg[:, k:k + 1] * _from_tiles(cur, k * tm, tm)
    o_ref[...] = _layer_norm(DN_ALPHA * x1_ref[...] + y, g_ref[...], b_ref[...])


def _combine(ys, dest_flat, x1, tg, ln_g, ln_b, tm):
    n = x1.shape[0]
    nsteps = n // tm
    kern = functools.partial(_combine_kernel, tm=tm)
    return pl.pallas_call(
        kern,
        grid=(nsteps,),
        in_specs=[pl.BlockSpec((tm * TOP_K,), lambda i: (i,), memory_space=pltpu.SMEM),
                  pl.BlockSpec((tm * TOP_K,), lambda i: (jnp.minimum(i + 1, nsteps - 1),),
                               memory_space=pltpu.SMEM),
                  pl.BlockSpec(memory_space=pl.ANY),
                  pl.BlockSpec((tm, D_MODEL), lambda i: (i, 0)),
                  pl.BlockSpec((tm, LANES), lambda i: (i, 0)),
                  pl.BlockSpec((1, D_MODEL), lambda i: (0, 0)),
                  pl.BlockSpec((1, D_MODEL), lambda i: (0, 0))],
        out_specs=pl.BlockSpec((tm, D_MODEL), lambda i: (i, 0)),
        out_shape=jax.ShapeDtypeStruct((n, D_MODEL), F32),
        scratch_shapes=[pltpu.VMEM((2, TOP_K * tm * SUBL, LANES), F32), pltpu.SemaphoreType.DMA((2,))],
        compiler_params=_cparams(("arbitrary",)),
        name="moe_combine",
    )(dest_flat, dest_flat, ys, x1, tg, ln_g, ln_b)


def _pad_lanes(v, width=LANES, value=0.0):
    return jnp.pad(v, ((0, 0), (0, width - v.shape[-1])), constant_values=value)


def _prep_w_in(w):
    body = w[:, :6 * SEC]
    gi = _pad_lanes(w[:, 6 * SEC:6 * SEC + ML_HEADS])
    gf = _pad_lanes(w[:, 6 * SEC + ML_HEADS:6 * SEC + 2 * ML_HEADS])
    return jnp.concatenate([body, gi, gf], axis=1).astype(BF16)


def _layer(x, p, big, layer, lam_init, cfg):
    n = x.shape[0]
    b, s = cfg["batch"], cfg["seq"]
    tm_e = cfg["tm_expert"]

    qkv, ml, gates = _inproj(x, _prep_w_in(p["w_in"]), cfg["tm_proj"])

    lamv = jnp.zeros((8, LANES), F32)
    for r, nm in enumerate(("lam_q1", "lam_k1", "lam_q2", "lam_k2")):
        lamv = lamv.at[r, :DA_QK_DIM].set(p[nm])
    da = _diff_attention(qkv.reshape(b, s, 3 * SEC), lamv, p["da_norm_g"].reshape(1, LANES),
                         lam_init, cfg["tq"], cfg["tk"])

    gate_b = jnp.concatenate([_pad_lanes(p["gate_b"][None, :ML_HEADS]),
                              _pad_lanes(p["gate_b"][None, ML_HEADS:])], axis=1)
    mlo = _mlstm(ml.reshape(b, s, 3 * SEC), gates.reshape(b, s, GATE_W), p["conv_w"],
                 p["conv_b"].reshape(1, SEC), gate_b, cfg["chunk"], cfg["ml_group"])

    wr = _pad_lanes(p["w_router"])
    wr_hi = wr.astype(BF16)
    wr = jnp.concatenate([wr_hi, (wr - wr_hi.astype(F32)).astype(BF16)], axis=1)
    br = _pad_lanes(p["b_router"][None, :], value=NEG_INF)
    x1, x1t, te, tg, rk, cnt = _outproj_router(
        da.reshape(n, SEC), mlo.reshape(n, SEC), x, p["w_out"].astype(BF16),
        p["ln1_g"].reshape(1, D_MODEL), p["ln1_b"].reshape(1, D_MODEL), wr, br, cfg["tm_proj"])

    counts = cnt[0, :N_EXPERTS]
    padded = (counts + tm_e - 1) // tm_e * tm_e
    pad_ends = jnp.cumsum(padded)
    pad_starts = pad_ends - padded
    top_e = te[:, :TOP_K]
    sel = top_e[:, :, None] == jnp.arange(N_EXPERTS, dtype=I32)[None, None, :]
    dest = rk[:, :TOP_K] + jnp.sum(jnp.where(sel, pad_starts[None, None, :], 0), axis=-1)
    dest_flat = dest.reshape(n * TOP_K).astype(I32)
    n_rows = n * TOP_K + N_EXPERTS * tm_e
    nb = n_rows // tm_e
    blk_start = jnp.arange(nb, dtype=I32) * tm_e
    block_e = jnp.minimum(jnp.sum(blk_start[:, None] >= pad_ends[None, :], axis=-1),
                          N_EXPERTS - 1).astype(I32)
    nused = (pad_ends[-1:] // tm_e).astype(I32)
    ids = jnp.where(padded > 0, jnp.arange(N_EXPERTS, dtype=I32), N_EXPERTS)
    later = jnp.concatenate([lax.cummin(ids[::-1])[::-1][1:], jnp.full((1,), N_EXPERTS, I32)])
    next_e = jnp.where(later >= N_EXPERTS, -1, later)[block_e].astype(I32)
    seg_lo = (pad_starts + counts).astype(I32)
    seg_hi = pad_ends.at[N_EXPERTS - 1].set(n_rows).astype(I32)

    xs = _dispatch(x1t, dest_flat, seg_lo, seg_hi, n_rows, cfg["tm_disp"])
    ys = _experts(xs, block_e, nused, next_e, big["w_gu"], big["b_gu"], big["w_down"],
                  big["b_down"], layer, tm_e)
    return _combine(ys, dest_flat, x1, tg, p["ln2_g"].reshape(1, D_MODEL),
                    p["ln2_b"].reshape(1, D_MODEL), cfg["tm_comb"])


def _forward(x, params, cfg):
    b, s, d = x.shape
    h = x.reshape(b * s, d)
    depth = params["w_in"].shape[0]
    big = dict(w_gu=params["w_gu"].reshape(depth * N_EXPERTS, D_MODEL, 2 * D_FF),
               b_gu=params["b_gu"].reshape(depth * N_EXPERTS, 1, 2 * D_FF),
               w_down=params["w_down"].reshape(depth * N_EXPERTS, D_FF, D_MODEL),
               b_down=params["b_down"].reshape(depth * N_EXPERTS, 1, D_MODEL))
    for l in range(depth):
        lam_init = 0.8 - 0.6 * math.exp(-0.3 * l)
        p = {k: v[l] for k, v in params.items() if k not in big}
        h = _layer(h, p, big, l, lam_init, cfg)
    return h.reshape(b, s, d)


def kernel(x, w_in, conv_w, conv_b, gate_b, lam_q1, lam_k1, lam_q2, lam_k2, da_norm_g, w_out,
           ln1_g, ln1_b, w_router, b_router, w_gu, b_gu, w_down, b_down, ln2_g, ln2_b):
    params = dict(w_in=w_in, conv_w=conv_w, conv_b=conv_b, gate_b=gate_b, lam_q1=lam_q1,
                  lam_k1=lam_k1, lam_q2=lam_q2, lam_k2=lam_k2, da_norm_g=da_norm_g, w_out=w_out,
                  ln1_g=ln1_g, ln1_b=ln1_b, w_router=w_router, b_router=b_router, w_gu=w_gu,
                  b_gu=b_gu, w_down=w_down, b_down=b_down, ln2_g=ln2_g, ln2_b=ln2_b)
    cfg = dict(batch=x.shape[0], seq=x.shape[1], tm_proj=512, tq=2048, tk=512, chunk=256, ml_group=2,
               tm_expert=512, tm_disp=256, tm_comb=256)
    return _forward(x, params, cfg)
```

```python
import functools
import math

import jax
import jax.numpy as jnp
from jax import lax
from jax.experimental import pallas as pl
from jax.experimental.pallas import tpu as pltpu

F32 = jnp.float32
BF16 = jnp.bfloat16
I32 = jnp.int32

D_MODEL = 1024
DEPTH = 4
DA_HEADS = 4
DA_QK_DIM = 64
DA_V_DIM = 128
ML_HEADS = 4
ML_QK_DIM = 64
ML_V_DIM = 128
CONV_WIDTH = 4
N_EXPERTS = 32
TOP_K = 4
D_FF = 1024
SWIGLU_LIMIT = 7.0
SWIGLU_ALPHA = 1.702
DN_ALPHA = (2 * DEPTH) ** 0.25
LN_EPS = 1e-5
RMS_EPS = 1e-5

LANES = 128
SEC = 512
GATE_W = 2 * LANES
IN_W_PAD = 6 * SEC + GATE_W
VMEM_LIMIT = 56 * 1024 * 1024
NEG_INF = float("-inf")
DA_Q_SCALE = DA_QK_DIM ** -0.5 * math.log2(math.e)


def _cparams(sem):
    return pltpu.CompilerParams(dimension_semantics=sem, vmem_limit_bytes=VMEM_LIMIT)


def _inproj_kernel(x_ref, w_ref, qkv_ref, ml_ref, gate_ref):
    xb = x_ref[...].astype(BF16)
    for c in range(3):
        sl = slice(c * SEC, (c + 1) * SEC)
        r = jnp.dot(xb, w_ref[:, sl], preferred_element_type=F32)
        if c == 0:
            r = r * DA_Q_SCALE
        qkv_ref[:, sl] = r.astype(BF16)
    for c in range(3):
        ml_ref[:, c * SEC:(c + 1) * SEC] = jnp.dot(
            xb, w_ref[:, (3 + c) * SEC:(4 + c) * SEC], preferred_element_type=F32)
    gate_ref[...] = jnp.dot(xb, w_ref[:, 6 * SEC:], preferred_element_type=F32)


def _inproj(x, w, tm):
    n = x.shape[0]
    return pl.pallas_call(
        _inproj_kernel,
        grid=(n // tm,),
        in_specs=[pl.BlockSpec((tm, D_MODEL), lambda i: (i, 0)),
                  pl.BlockSpec((D_MODEL, IN_W_PAD), lambda i: (0, 0))],
        out_specs=[pl.BlockSpec((tm, 3 * SEC), lambda i: (i, 0)),
                   pl.BlockSpec((tm, 3 * SEC), lambda i: (i, 0)),
                   pl.BlockSpec((tm, GATE_W), lambda i: (i, 0))],
        out_shape=[jax.ShapeDtypeStruct((n, 3 * SEC), BF16),
                   jax.ShapeDtypeStruct((n, 3 * SEC), F32),
                   jax.ShapeDtypeStruct((n, GATE_W), F32)],
        compiler_params=_cparams(("parallel",)),
        name="in_proj",
    )(x, w)


def _attn_kernel(qi_tab, ki_tab, q_ref, k_ref, v_ref, lamv_ref, g_ref, o_ref, qs, m_s, acc,
                 *, lam_init, tq, tk):
    step = pl.program_id(2)
    qi = qi_tab[step]
    ki = ki_tab[step]
    ng = tq // tk
    reps = tk // LANES

    @pl.when(ki == 0)
    def _():
        q = q_ref[0]
        lane = lax.broadcasted_iota(I32, q.shape, 1)
        zero = jnp.zeros_like(q)
        qs[0:tq, :] = jnp.where(lane < DA_QK_DIM, q, zero)
        qs[tq:2 * tq, :] = jnp.where(lane >= DA_QK_DIM, q, zero)
        m_s[...] = jnp.full(m_s.shape, NEG_INF, F32)
        acc[...] = jnp.zeros(acc.shape, F32)

    def update(diag):
        k = k_ref[0]
        v_aug = jnp.concatenate([v_ref[0], jnp.ones((tk, LANES), BF16)], axis=1)
        groups = [(gi, slice(mp * tq + gi * tk, mp * tq + (gi + 1) * tk))
                  for gi in range(ng) if gi >= diag for mp in range(2)]
        ss = [lax.dot_general(qs[g, :], k, (((1,), (1,)), ((), ())), preferred_element_type=F32)
              for _, g in groups]
        if diag >= 0:
            row = lax.broadcasted_iota(I32, (tk, tk), 0)
            col = lax.broadcasted_iota(I32, (tk, tk), 1)
            ss = [jnp.where(row >= col, s, NEG_INF) if gi == diag else s
                  for (gi, _), s in zip(groups, ss)]
        for (_, g), s in zip(groups, ss):
            m_old = m_s[g, :]
            m_new = jnp.maximum(m_old, jnp.max(s, axis=-1, keepdims=True))
            p = jnp.exp2(s - jnp.tile(m_new, (1, reps)))
            alpha = jnp.exp2(m_old - m_new)
            pv = jnp.dot(p.astype(BF16), v_aug, preferred_element_type=F32)
            acc[g, :] = jnp.tile(alpha, (1, 2)) * acc[g, :] + pv
            m_s[g, :] = m_new

    @pl.when(ki < qi * ng)
    def _():
        update(-1)

    for d in range(ng):
        @pl.when(ki == qi * ng + d)
        def _(d=d):
            update(d)

    @pl.when(ki == qi * ng + ng - 1)
    def _():
        lamv = lamv_ref[...]
        lam = (jnp.exp(jnp.sum(lamv[0:1] * lamv[1:2], axis=-1, keepdims=True))
               - jnp.exp(jnp.sum(lamv[2:3] * lamv[3:4], axis=-1, keepdims=True)) + lam_init)
        on = acc[:, 0:LANES] / acc[:, LANES:2 * LANES]
        o = on[0:tq] - lam * on[tq:2 * tq]
        ms = jnp.mean(o * o, axis=-1, keepdims=True)
        o = o * lax.rsqrt(ms + RMS_EPS) * g_ref[...] * (1.0 - lam_init)
        o_ref[0] = o.astype(o_ref.dtype)


def _diff_attention(qkv, lamv, norm_g, lam_init, tq, tk):
    b, s, _ = qkv.shape
    ng = tq // tk
    pairs = [(qi, ki) for qi in range(s // tq) for ki in range((qi + 1) * ng)]
    qi_tab = jnp.asarray([p[0] for p in pairs], I32)
    ki_tab = jnp.asarray([p[1] for p in pairs], I32)
    kern = functools.partial(_attn_kernel, lam_init=lam_init, tq=tq, tk=tk)
    grid_spec = pltpu.PrefetchScalarGridSpec(
        num_scalar_prefetch=2,
        grid=(b, DA_HEADS, len(pairs)),
        in_specs=[
            pl.BlockSpec((1, tq, LANES), lambda bi, h, t, qt, kt: (bi, qt[t], h)),
            pl.BlockSpec((1, tk, LANES), lambda bi, h, t, qt, kt: (bi, kt[t], DA_HEADS + h)),
            pl.BlockSpec((1, tk, LANES), lambda bi, h, t, qt, kt: (bi, kt[t], 2 * DA_HEADS + h)),
            pl.BlockSpec((8, LANES), lambda bi, h, t, qt, kt: (0, 0)),
            pl.BlockSpec((1, LANES), lambda bi, h, t, qt, kt: (0, 0)),
        ],
        out_specs=pl.BlockSpec((1, tq, LANES), lambda bi, h, t, qt, kt: (bi, qt[t], h)),
        scratch_shapes=[pltpu.VMEM((2 * tq, LANES), BF16), pltpu.VMEM((2 * tq, LANES), F32),
                        pltpu.VMEM((2 * tq, 2 * LANES), F32)],
    )
    return pl.pallas_call(
        kern,
        grid_spec=grid_spec,
        out_shape=jax.ShapeDtypeStruct((b, s, SEC), BF16),
        compiler_params=_cparams(("parallel", "parallel", "arbitrary")),
        name="diff_attn",
    )(qi_tab, ki_tab, qkv, qkv, qkv, lamv, norm_g)


def _log_sigmoid(x):
    return jnp.minimum(x, 0.0) - jnp.log1p(jnp.exp(-jnp.abs(x)))


def _sigmoid(x):
    return 1.0 / (1.0 + jnp.exp(-x))


def _mlstm_kernel(ml_ref, gt_ref, cw_ref, cb_ref, gb_ref, o_ref, xbuf, cst, mst, *, chunk, group):
    @pl.when(pl.program_id(1) == 0)
    def _():
        xbuf[:, 0:ML_HALO, :] = jnp.zeros((group, ML_HALO, SEC), F32)
        cst[...] = jnp.zeros(cst.shape, F32)
        mst[...] = jnp.full(mst.shape, NEG_INF, F32)

    for bb in range(group):
        _mlstm_chunk(bb, ml_ref, gt_ref, cw_ref, cb_ref, gb_ref, o_ref, xbuf, cst, mst, chunk)


ML_HALO = 8


def _mlstm_chunk(bb, ml_ref, gt_ref, cw_ref, cb_ref, gb_ref, o_ref, xbuf, cst, mst, chunk):
    L = chunk
    HALO = ML_HALO

    xbuf[bb, HALO:HALO + L, :] = ml_ref[bb, :, 0:SEC]
    y = cb_ref[...] + cw_ref[CONV_WIDTH - 1:CONV_WIDTH, :] * xbuf[bb, HALO:HALO + L, :]
    for j in range(CONV_WIDTH - 1):
        off = HALO - (CONV_WIDTH - 1) + j
        y = y + cw_ref[j:j + 1, :] * xbuf[bb, off:off + L, :]
    xbuf[bb, 0:HALO, :] = xbuf[bb, L:L + HALO, :]
    qk = y * _sigmoid(y)

    gi = gt_ref[bb, :, 0:LANES] + gb_ref[:, 0:LANES]
    lf = _log_sigmoid(gt_ref[bb, :, LANES:GATE_W] + gb_ref[:, LANES:GATE_W])
    row = lax.broadcasted_iota(I32, (L, L), 0)
    col = lax.broadcasted_iota(I32, (L, L), 1)
    causal = row >= col
    bcum = jnp.dot(causal.astype(F32), lf, preferred_element_type=F32,
                   precision=lax.Precision.HIGHEST)
    g = bcum[L - 1:L, :]
    w_end = g - bcum + gi
    m_loc = jnp.max(w_end, axis=0, keepdims=True)
    e_end = jnp.exp(w_end - m_loc)
    m_prev = mst[bb]
    m_new = jnp.maximum(g + m_prev, m_loc)
    a_dec = jnp.exp(g + m_prev - m_new)
    b_dec = jnp.exp(m_loc - m_new)
    mst[bb] = m_new
    inter_log = bcum + m_prev
    bcum_t = bcum.T
    gi_t = gi.T

    lane = lax.broadcasted_iota(I32, (L, LANES), 1)
    sub = lax.broadcasted_iota(I32, (LANES, 1), 0)
    ones_col = jnp.where(lane == 0, 1.0, 0.0).astype(BF16)
    for p in range(ML_HEADS // 2):
        q_pair = qk[:, p * LANES:(p + 1) * LANES] * (ML_QK_DIM ** -0.5)
        k_pair = qk[:, 2 * LANES + p * LANES:2 * LANES + (p + 1) * LANES]
        kb = k_pair.astype(BF16)
        cp = bb * (ML_HEADS // 2) + p
        c_prev = cst[cp].astype(BF16)
        upd = jnp.zeros((LANES, 2 * LANES), F32)
        for hh in range(2):
            h = 2 * p + hh
            head_lanes = (lane >= hh * ML_QK_DIM) & (lane < (hh + 1) * ML_QK_DIM)
            qm = jnp.where(head_lanes, q_pair, 0.0).astype(BF16)
            s = lax.dot_general(qm, kb, (((1,), (1,)), ((), ())), preferred_element_type=F32)
            d = jnp.where(causal, bcum[:, h:h + 1] - bcum_t[h:h + 1, :] + gi_t[h:h + 1, :], NEG_INF)
            il = inter_log[:, h:h + 1]
            m_t = jnp.maximum(il, jnp.max(d, axis=-1, keepdims=True))
            sd = (s * jnp.exp(d - m_t)).astype(BF16)
            inter_w = jnp.exp(il - m_t)
            v_aug = jnp.concatenate(
                [ml_ref[bb, :, SEC + h * LANES:SEC + (h + 1) * LANES].astype(BF16), ones_col], axis=1)
            intra = jnp.dot(sd, v_aug, preferred_element_type=F32)
            inter = jnp.dot(qm, c_prev, preferred_element_type=F32)
            num = inter_w * inter[:, 0:LANES] + intra[:, 0:LANES]
            den = inter_w * inter[:, LANES:LANES + 1] + intra[:, LANES:LANES + 1]
            hid = num / jnp.maximum(jnp.abs(den), jnp.exp(-m_t))
            o_gate = ml_ref[bb, :, 2 * SEC + h * LANES:2 * SEC + (h + 1) * LANES]
            o_ref[bb, :, h * LANES:(h + 1) * LANES] = (hid * _sigmoid(o_gate)).astype(o_ref.dtype)
            ek = jnp.where(head_lanes, e_end[:, h:h + 1] * k_pair, 0.0).astype(BF16)
            upd = upd + lax.dot_general(ek, v_aug, (((0,), (0,)), ((), ())),
                                        preferred_element_type=F32)
        first = sub < ML_QK_DIM
        a_rows = jnp.where(first, a_dec[:, 2 * p:2 * p + 1], a_dec[:, 2 * p + 1:2 * p + 2])
        b_rows = jnp.where(first, b_dec[:, 2 * p:2 * p + 1], b_dec[:, 2 * p + 1:2 * p + 2])
        cst[cp] = a_rows * cst[cp] + b_rows * upd


def _mlstm(ml, gates, conv_w, conv_b, gate_b, chunk, group):
    b, s, _ = ml.shape
    kern = functools.partial(_mlstm_kernel, chunk=chunk, group=group)
    return pl.pallas_call(
        kern,
        grid=(b // group, s // chunk),
        in_specs=[pl.BlockSpec((group, chunk, 3 * SEC), lambda bi, c: (bi, c, 0)),
                  pl.BlockSpec((group, chunk, GATE_W), lambda bi, c: (bi, c, 0)),
                  pl.BlockSpec((CONV_WIDTH, SEC), lambda bi, c: (0, 0)),
                  pl.BlockSpec((1, SEC), lambda bi, c: (0, 0)),
                  pl.BlockSpec((1, GATE_W), lambda bi, c: (0, 0))],
        out_specs=pl.BlockSpec((group, chunk, SEC), lambda bi, c: (bi, c, 0)),
        out_shape=jax.ShapeDtypeStruct((b, s, SEC), BF16),
        scratch_shapes=[pltpu.VMEM((group, chunk + 8, SEC), F32),
                        pltpu.VMEM((group * (ML_HEADS // 2), LANES, 2 * LANES), F32),
                        pltpu.VMEM((group, 1, LANES), F32)],
        compiler_params=_cparams(("parallel", "arbitrary")),
        name="mlstm",
    )(ml, gates, conv_w, conv_b, gate_b)


SUBL = D_MODEL // LANES


def _to_tiles(ref, val, base=0):
    rows = val.shape[0]
    for s in range(SUBL):
        ref[pl.ds(base * SUBL + s, rows, stride=SUBL), :] = val[:, s * LANES:(s + 1) * LANES]


def _tile(ref, t):
    return ref.at[pl.ds(pl.multiple_of(t * SUBL, SUBL), SUBL), :]


def _from_tiles(ref, base, rows):
    return jnp.concatenate(
        [ref[pl.ds(base * SUBL + s, rows, stride=SUBL), :] for s in range(SUBL)], axis=1)


def _layer_norm(h, g, b):
    mu = jnp.mean(h, axis=-1, keepdims=True)
    hc = h - mu
    var = jnp.mean(hc * hc, axis=-1, keepdims=True)
    return hc * lax.rsqrt(var + LN_EPS) * g + b


def _outproj_router_kernel(da_ref, mlo_ref, x_ref, wo_ref, g_ref, b_ref, wr_ref, br_ref,
                           x1_ref, x1t_ref, te_ref, tg_ref, rk_ref, cnt_ref, carry, *, tm):
    i = pl.program_id(0)

    @pl.when(i == 0)
    def _():
        carry[...] = jnp.zeros(carry.shape, F32)

    mix = (jnp.dot(da_ref[...], wo_ref[0:SEC, :], preferred_element_type=F32)
           + jnp.dot(mlo_ref[...], wo_ref[SEC:2 * SEC, :], preferred_element_type=F32))
    x1 = _layer_norm(DN_ALPHA * x_ref[...] + mix, g_ref[...], b_ref[...])
    x1_ref[...] = x1
    _to_tiles(x1t_ref, x1)

    x_hi = x1.astype(BF16)
    x_lo = (x1 - x_hi.astype(F32)).astype(BF16)
    hh_hl = jnp.dot(x_hi, wr_ref[...], preferred_element_type=F32)
    lh = jnp.dot(x_lo, wr_ref[:, 0:LANES], preferred_element_type=F32)
    logits = hh_hl[:, 0:LANES] + (hh_hl[:, LANES:2 * LANES] + lh) + br_ref[...]
    lane = lax.broadcasted_iota(I32, (tm, LANES), 1)
    lane_f = lane.astype(F32)
    work = logits
    hot = []
    vals = []
    idxs = []
    for _k in range(TOP_K):
        mx = jnp.max(work, axis=-1, keepdims=True)
        idx = jnp.min(jnp.where(work == mx, lane_f, float(LANES)), axis=-1, keepdims=True)
        sel = lane_f == idx
        hot.append(sel)
        vals.append(mx)
        idxs.append(idx)
        work = jnp.where(sel, NEG_INF, work)
    exps = [jnp.exp(v - vals[0]) for v in vals]
    inv = 1.0 / (exps[0] + exps[1] + exps[2] + exps[3])

    onehot = (hot[0] | hot[1] | hot[2] | hot[3])
    oh = jnp.where(onehot, 1.0, 0.0)
    row = lax.broadcasted_iota(I32, (tm, tm), 0)
    col = lax.broadcasted_iota(I32, (tm, tm), 1)
    before = jnp.dot((row > col).astype(BF16), oh.astype(BF16), preferred_element_type=F32)
    pos = before + carry[...]
    te = jnp.zeros((tm, LANES), I32)
    tg = jnp.zeros((tm, LANES), F32)
    rk = jnp.zeros((tm, LANES), I32)
    for k in range(TOP_K):
        slot = lane == k
        r_k = jnp.sum(jnp.where(hot[k], pos, 0.0), axis=-1, keepdims=True)
        te = jnp.where(slot, idxs[k].astype(I32), te)
        tg = jnp.where(slot, exps[k] * inv, tg)
        rk = jnp.where(slot, r_k.astype(I32), rk)
    te_ref[...] = te
    tg_ref[...] = tg
    rk_ref[...] = rk
    carry[...] = carry[...] + jnp.sum(oh, axis=0, keepdims=True)
    cnt_ref[...] = jnp.broadcast_to(carry[...], cnt_ref.shape).astype(I32)


def _outproj_router(da, mlo, x, wo, ln_g, ln_b, wr, br, tm):
    n = x.shape[0]
    kern = functools.partial(_outproj_router_kernel, tm=tm)
    row_blk = lambda w: pl.BlockSpec((tm, w), lambda i: (i, 0))
    const = lambda r, w: pl.BlockSpec((r, w), lambda i: (0, 0))
    return pl.pallas_call(
        kern,
        grid=(n // tm,),
        in_specs=[row_blk(SEC), row_blk(SEC), row_blk(D_MODEL), const(2 * SEC, D_MODEL),
                  const(1, D_MODEL), const(1, D_MODEL), const(D_MODEL, 2 * LANES), const(1, LANES)],
        out_specs=[row_blk(D_MODEL), pl.BlockSpec((tm * SUBL, LANES), lambda i: (i, 0)),
                   row_blk(LANES), row_blk(LANES), row_blk(LANES), const(8, LANES)],
        out_shape=[jax.ShapeDtypeStruct((n, D_MODEL), F32),
                   jax.ShapeDtypeStruct((n * SUBL, LANES), F32),
                   jax.ShapeDtypeStruct((n, LANES), I32),
                   jax.ShapeDtypeStruct((n, LANES), F32),
                   jax.ShapeDtypeStruct((n, LANES), I32),
                   jax.ShapeDtypeStruct((8, LANES), I32)],
        scratch_shapes=[pltpu.VMEM((1, LANES), F32)],
        compiler_params=_cparams(("arbitrary",)),
        name="outproj_router",
    )(da, mlo, x, wo, ln_g, ln_b, wr, br)


def _dispatch_kernel(dest_ref, lo_ref, hi_ref, x_ref, xs_ref, zeros, sem, zsem, *, tm):
    i = pl.program_id(0)

    def issue(t, carry):
        for k in range(TOP_K):
            pltpu.make_async_copy(_tile(x_ref, t), _tile(xs_ref, dest_ref[t * TOP_K + k]),
                                  sem).start(priority=k % 2)
        return carry

    lax.fori_loop(0, tm, issue, 0)

    @pl.when(i == 0)
    def _():
        zeros[...] = jnp.zeros(zeros.shape, F32)

        def per_expert(e, carry):
            def fill(r, c):
                pltpu.make_async_copy(zeros, _tile(xs_ref, r), zsem).start()
                return c

            def drain(r, c):
                pltpu.make_async_copy(zeros, _tile(xs_ref, r), zsem).wait()
                return c

            lax.fori_loop(lo_ref[e], hi_ref[e], fill, 0)
            lax.fori_loop(lo_ref[e], hi_ref[e], drain, 0)
            return carry

        lax.fori_loop(0, N_EXPERTS, per_expert, 0)

    for k in range(TOP_K):
        pltpu.make_async_copy(x_ref, xs_ref.at[pl.ds(0, tm * SUBL), :], sem).wait()


def _dispatch(x1t, dest_flat, seg_lo, seg_hi, n_rows, tm):
    n = x1t.shape[0] // SUBL
    kern = functools.partial(_dispatch_kernel, tm=tm)
    return pl.pallas_call(
        kern,
        grid=(n // tm,),
        in_specs=[pl.BlockSpec((tm * TOP_K,), lambda i: (i,), memory_space=pltpu.SMEM),
                  pl.BlockSpec(memory_space=pltpu.SMEM),
                  pl.BlockSpec(memory_space=pltpu.SMEM),
                  pl.BlockSpec((tm * SUBL, LANES), lambda i: (i, 0))],
        out_specs=pl.BlockSpec(memory_space=pl.ANY),
        out_shape=jax.ShapeDtypeStruct((n_rows * SUBL, LANES), F32),
        scratch_shapes=[pltpu.VMEM((SUBL, LANES), F32), pltpu.SemaphoreType.DMA(()),
                        pltpu.SemaphoreType.DMA(())],
        compiler_params=_cparams(("arbitrary",)),
        name="moe_dispatch",
    )(dest_flat, seg_lo, seg_hi, x1t)


FF_CHUNK = 512


def _expert_kernel(be_ref, nused_ref, nxt_ref, xs_ref, wgu_hbm, bgu_ref, wd_hbm, bd_ref, ys_ref,
                   wgu_b, wd_b, wgu_f, wd_f, wsem, *, tm, layer):
    i = pl.program_id(0)
    live = i < nused_ref[0]
    fresh = jnp.logical_or(i == 0, be_ref[i] != be_ref[jnp.maximum(i - 1, 0)])

    def weight_copies(e):
        w = layer * N_EXPERTS + e
        return (pltpu.make_async_copy(wgu_hbm.at[w], wgu_f, wsem.at[0]),
                pltpu.make_async_copy(wd_hbm.at[w], wd_f, wsem.at[1]))

    @pl.when(i == 0)
    def _():
        for cp in weight_copies(be_ref[0]):
            cp.start()

    @pl.when(jnp.logical_and(live, fresh))
    def _():
        for cp in weight_copies(be_ref[i]):
            cp.wait()
        wgu_b[...] = wgu_f[...].astype(BF16)
        wd_b[...] = wd_f[...].astype(BF16)

        @pl.when(nxt_ref[i] >= 0)
        def _():
            for cp in weight_copies(nxt_ref[i]):
                cp.start()

    @pl.when(live)
    def _():
        x16 = _from_tiles(xs_ref, 0, tm).astype(BF16)
        y = None
        for c in range(D_FF // FF_CHUNK):
            lo, hi = c * FF_CHUNK, (c + 1) * FF_CHUNK
            hg = jnp.dot(x16, wgu_b[:, lo:hi], preferred_element_type=F32) + bgu_ref[0, :, lo:hi]
            hu = (jnp.dot(x16, wgu_b[:, D_FF + lo:D_FF + hi], preferred_element_type=F32)
                  + bgu_ref[0, :, D_FF + lo:D_FF + hi])
            gate = jnp.minimum(hg, SWIGLU_LIMIT)
            up = jnp.clip(hu, -SWIGLU_LIMIT, SWIGLU_LIMIT)
            act = ((up + 1.0) * (gate * _sigmoid(SWIGLU_ALPHA * gate))).astype(BF16)
            part = jnp.dot(act, wd_b[lo:hi, :], preferred_element_type=F32)
            y = part if y is None else y + part
        _to_tiles(ys_ref, y + bd_ref[0])

    @pl.when(jnp.logical_not(live))
    def _():
        ys_ref[...] = jnp.zeros(ys_ref.shape, F32)


def _experts(xs, block_e, nused, next_e, wgu, bgu, wd, bd, layer, tm):
    n_rows = xs.shape[0] // SUBL
    nb = n_rows // tm

    def rows(i, be, nu, nx):
        return (jnp.minimum(i, nu[0] - 1), 0)

    def per_e(i, be, nu, nx):
        return (layer * N_EXPERTS + be[i], 0, 0)

    grid_spec = pltpu.PrefetchScalarGridSpec(
        num_scalar_prefetch=3,
        grid=(nb,),
        in_specs=[pl.BlockSpec((tm * SUBL, LANES), rows),
                  pl.BlockSpec(memory_space=pl.ANY),
                  pl.BlockSpec((1, 1, 2 * D_FF), per_e),
                  pl.BlockSpec(memory_space=pl.ANY),
                  pl.BlockSpec((1, 1, D_MODEL), per_e)],
        out_specs=pl.BlockSpec((tm * SUBL, LANES), lambda i, be, nu, nx: (i, 0)),
        scratch_shapes=[pltpu.VMEM((D_MODEL, 2 * D_FF), BF16), pltpu.VMEM((D_FF, D_MODEL), BF16),
                        pltpu.VMEM((D_MODEL, 2 * D_FF), F32), pltpu.VMEM((D_FF, D_MODEL), F32),
                        pltpu.SemaphoreType.DMA((2,))],
    )
    return pl.pallas_call(
        functools.partial(_expert_kernel, tm=tm, layer=layer),
        grid_spec=grid_spec,
        out_shape=jax.ShapeDtypeStruct((n_rows * SUBL, LANES), F32),
        compiler_params=_cparams(("arbitrary",)),
        name="moe_experts",
    )(block_e, nused, next_e, xs, wgu, bgu, wd, bd)


def _combine_kernel(dcur_ref, dnext_ref, ys_ref, x1_ref, tg_ref, g_ref, b_ref, o_ref, buf, sem, *, tm):
    i = pl.program_id(0)
    last = pl.num_programs(0) - 1
    p = i % 2
    rows = TOP_K * tm * SUBL

    def fetch(dest_ref, slot):
        def issue(t, carry):
            for k in range(TOP_K):
                pltpu.make_async_copy(_tile(ys_ref, dest_ref[t * TOP_K + k]),
                                      _tile(buf.at[slot], k * tm + t), sem.at[slot]).start(priority=k % 2)
            return carry

        lax.fori_loop(0, tm, issue, 0)

    @pl.when(i == 0)
    def _():
        fetch(dcur_ref, 0)

    @pl.when(i < last)
    def _():
        fetch(dnext_ref, 1 - p)

    pltpu.make_async_copy(ys_ref.at[pl.ds(0, rows), :], buf.at[p], sem.at[p]).wait()
    tg = tg_ref[...]
    cur = buf.at[p]
    y = tg[:, 0:1] * _from_tiles(cur, 0, tm)
    for k in range(1, TOP_K):
        y = y + tg[:, k:k + 1] * _from_tiles(cur, k * tm, tm)
    o_ref[...] = _layer_norm(DN_ALPHA * x1_ref[...] + y, g_ref[...], b_ref[...])


def _combine(ys, dest_flat, x1, tg, ln_g, ln_b, tm):
    n = x1.shape[0]
    nsteps = n // tm
    kern = functools.partial(_combine_kernel, tm=tm)
    return pl.pallas_call(
        kern,
        grid=(nsteps,),
        in_specs=[pl.BlockSpec((tm * TOP_K,), lambda i: (i,), memory_space=pltpu.SMEM),
                  pl.BlockSpec((tm * TOP_K,), lambda i: (jnp.minimum(i + 1, nsteps - 1),),
                               memory_space=pltpu.SMEM),
                  pl.BlockSpec(memory_space=pl.ANY),
                  pl.BlockSpec((tm, D_MODEL), lambda i: (i, 0)),
                  pl.BlockSpec((tm, LANES), lambda i: (i, 0)),
                  pl.BlockSpec((1, D_MODEL), lambda i: (0, 0)),
                  pl.BlockSpec((1, D_MODEL), lambda i: (0, 0))],
        out_specs=pl.BlockSpec((tm, D_MODEL), lambda i: (i, 0)),
        out_shape=jax.ShapeDtypeStruct((n, D_MODEL), F32),
        scratch_shapes=[pltpu.VMEM((2, TOP_K * tm * SUBL, LANES), F32), pltpu.SemaphoreType.DMA((2,))],
        compiler_params=_cparams(("arbitrary",)),
        name="moe_combine",
    )(dest_flat, dest_flat, ys, x1, tg, ln_g, ln_b)


def _pad_lanes(v, width=LANES, value=0.0):
    return jnp.pad(v, ((0, 0), (0, width - v.shape[-1])), constant_values=value)


def _prep_w_in(w):
    body = w[:, :6 * SEC]
    gi = _pad_lanes(w[:, 6 * SEC:6 * SEC + ML_HEADS])
    gf = _pad_lanes(w[:, 6 * SEC + ML_HEADS:6 * SEC + 2 * ML_HEADS])
    return jnp.concatenate([body, gi, gf], axis=1).astype(BF16)


def _layer(x, p, big, layer, lam_init, cfg):
    n = x.shape[0]
    b, s = cfg["batch"], cfg["seq"]
    tm_e = cfg["tm_expert"]

    qkv, ml, gates = _inproj(x, _prep_w_in(p["w_in"]), cfg["tm_proj"])

    lamv = jnp.zeros((8, LANES), F32)
    for r, nm in enumerate(("lam_q1", "lam_k1", "lam_q2", "lam_k2")):
        lamv = lamv.at[r, :DA_QK_DIM].set(p[nm])
    da = _diff_attention(qkv.reshape(b, s, 3 * SEC), lamv, p["da_norm_g"].reshape(1, LANES),
                         lam_init, cfg["tq"], cfg["tk"])

    gate_b = jnp.concatenate([_pad_lanes(p["gate_b"][None, :ML_HEADS]),
                              _pad_lanes(p["gate_b"][None, ML_HEADS:])], axis=1)
    mlo = _mlstm(ml.reshape(b, s, 3 * SEC), gates.reshape(b, s, GATE_W), p["conv_w"],
                 p["conv_b"].reshape(1, SEC), gate_b, cfg["chunk"], cfg["ml_group"])

    wr = _pad_lanes(p["w_router"])
    wr_hi = wr.astype(BF16)
    wr = jnp.concatenate([wr_hi, (wr - wr_hi.astype(F32)).astype(BF16)], axis=1)
    br = _pad_lanes(p["b_router"][None, :], value=NEG_INF)
    x1, x1t, te, tg, rk, cnt = _outproj_router(
        da.reshape(n, SEC), mlo.reshape(n, SEC), x, p["w_out"].astype(BF16),
        p["ln1_g"].reshape(1, D_MODEL), p["ln1_b"].reshape(1, D_MODEL), wr, br, cfg["tm_proj"])

    counts = cnt[0, :N_EXPERTS]
    padded = (counts + tm_e - 1) // tm_e * tm_e
    pad_ends = jnp.cumsum(padded)
    pad_starts = pad_ends - padded
    top_e = te[:, :TOP_K]
    sel = top_e[:, :, None] == jnp.arange(N_EXPERTS, dtype=I32)[None, None, :]
    dest = rk[:, :TOP_K] + jnp.sum(jnp.where(sel, pad_starts[None, None, :], 0), axis=-1)
    dest_flat = dest.reshape(n * TOP_K).astype(I32)
    n_rows = n * TOP_K + N_EXPERTS * tm_e
    nb = n_rows // tm_e
    blk_start = jnp.arange(nb, dtype=I32) * tm_e
    block_e = jnp.minimum(jnp.sum(blk_start[:, None] >= pad_ends[None, :], axis=-1),
                          N_EXPERTS - 1).astype(I32)
    nused = (pad_ends[-1:] // tm_e).astype(I32)
    ids = jnp.where(padded > 0, jnp.arange(N_EXPERTS, dtype=I32), N_EXPERTS)
    later = jnp.concatenate([lax.cummin(ids[::-1])[::-1][1:], jnp.full((1,), N_EXPERTS, I32)])
    next_e = jnp.where(later >= N_EXPERTS, -1, later)[block_e].astype(I32)
    seg_lo = (pad_starts + counts).astype(I32)
    seg_hi = pad_ends.at[N_EXPERTS - 1].set(n_rows).astype(I32)

    xs = _dispatch(x1t, dest_flat, seg_lo, seg_hi, n_rows, cfg["tm_disp"])
    ys = _experts(xs, block_e, nused, next_e, big["w_gu"], big["b_gu"], big["w_down"],
                  big["b_down"], layer, tm_e)
    return _combine(ys, dest_flat, x1, tg, p["ln2_g"].reshape(1, D_MODEL),
                    p["ln2_b"].reshape(1, D_MODEL), cfg["tm_comb"])


def _forward(x, params, cfg):
    b, s, d = x.shape
    h = x.reshape(b * s, d)
    depth = params["w_in"].shape[0]
    big = dict(w_gu=params["w_gu"].reshape(depth * N_EXPERTS, D_MODEL, 2 * D_FF),
               b_gu=params["b_gu"].reshape(depth * N_EXPERTS, 1, 2 * D_FF),
               w_down=params["w_down"].reshape(depth * N_EXPERTS, D_FF, D_MODEL),
               b_down=params["b_down"].reshape(depth * N_EXPERTS, 1, D_MODEL))
    for l in range(depth):
        lam_init = 0.8 - 0.6 * math.exp(-0.3 * l)
        p = {k: v[l] for k, v in params.items() if k not in big}
        h = _layer(h, p, big, l, lam_init, cfg)
    return h.reshape(b, s, d)


def kernel(x, w_in, conv_w, conv_b, gate_b, lam_q1, lam_k1, lam_q2, lam_k2, da_norm_g, w_out,
           ln1_g, ln1_b, w_router, b_router, w_gu, b_gu, w_down, b_down, ln2_g, ln2_b):
    params = dict(w_in=w_in, conv_w=conv_w, conv_b=conv_b, gate_b=gate_b, lam_q1=lam_q1,
                  lam_k1=lam_k1, lam_q2=lam_q2, lam_k2=lam_k2, da_norm_g=da_norm_g, w_out=w_out,
                  ln1_g=ln1_g, ln1_b=ln1_b, w_router=w_router, b_router=b_router, w_gu=w_gu,
                  b_gu=b_gu, w_down=w_down, b_down=b_down, ln2_g=ln2_g, ln2_b=ln2_b)
    cfg = dict(batch=x.shape[0], seq=x.shape[1], tm_proj=512, tq=2048, tk=512, chunk=256, ml_group=2,
               tm_expert=512, tm_disp=256, tm_comb=256)
    return _forward(x, params, cfg)
```

```python
import functools
import math

import jax
import jax.numpy as jnp
from jax import lax
from jax.experimental import pallas as pl
from jax.experimental.pallas import tpu as pltpu

F32 = jnp.float32
BF16 = jnp.bfloat16
I32 = jnp.int32

D_MODEL = 1024
DEPTH = 4
DA_HEADS = 4
DA_QK_DIM = 64
DA_V_DIM = 128
ML_HEADS = 4
ML_QK_DIM = 64
ML_V_DIM = 128
CONV_WIDTH = 4
N_EXPERTS = 32
TOP_K = 4
D_FF = 1024
SWIGLU_LIMIT = 7.0
SWIGLU_ALPHA = 1.702
DN_ALPHA = (2 * DEPTH) ** 0.25
LN_EPS = 1e-5
RMS_EPS = 1e-5

LANES = 128
SEC = 512
GATE_W = 2 * LANES
IN_W_PAD = 6 * SEC + GATE_W
VMEM_LIMIT = 56 * 1024 * 1024
NEG_INF = float("-inf")
DA_Q_SCALE = DA_QK_DIM ** -0.5 * math.log2(math.e)


def _cparams(sem):
    return pltpu.CompilerParams(dimension_semantics=sem, vmem_limit_bytes=VMEM_LIMIT)


def _inproj_kernel(x_ref, w_ref, qkv_ref, ml_ref, gate_ref):
    xb = x_ref[...].astype(BF16)
    for c in range(3):
        sl = slice(c * SEC, (c + 1) * SEC)
        r = jnp.dot(xb, w_ref[:, sl], preferred_element_type=F32)
        if c == 0:
            r = r * DA_Q_SCALE
        qkv_ref[:, sl] = r.astype(BF16)
    for c in range(3):
        ml_ref[:, c * SEC:(c + 1) * SEC] = jnp.dot(
            xb, w_ref[:, (3 + c) * SEC:(4 + c) * SEC], preferred_element_type=F32)
    gate_ref[...] = jnp.dot(xb, w_ref[:, 6 * SEC:], preferred_element_type=F32)


def _inproj(x, w, tm):
    n = x.shape[0]
    return pl.pallas_call(
        _inproj_kernel,
        grid=(n // tm,),
        in_specs=[pl.BlockSpec((tm, D_MODEL), lambda i: (i, 0)),
                  pl.BlockSpec((D_MODEL, IN_W_PAD), lambda i: (0, 0))],
        out_specs=[pl.BlockSpec((tm, 3 * SEC), lambda i: (i, 0)),
                   pl.BlockSpec((tm, 3 * SEC), lambda i: (i, 0)),
                   pl.BlockSpec((tm, GATE_W), lambda i: (i, 0))],
        out_shape=[jax.ShapeDtypeStruct((n, 3 * SEC), BF16),
                   jax.ShapeDtypeStruct((n, 3 * SEC), F32),
                   jax.ShapeDtypeStruct((n, GATE_W), F32)],
        compiler_params=_cparams(("parallel",)),
        name="in_proj",
    )(x, w)


def _attn_kernel(qi_tab, ki_tab, q_ref, k_ref, v_ref, lamv_ref, g_ref, o_ref, qs, m_s, acc,
                 *, lam_init, tq, tk):
    step = pl.program_id(2)
    qi = qi_tab[step]
    ki = ki_tab[step]
    ng = tq // tk
    reps = tk // LANES

    @pl.when(ki == 0)
    def _():
        q = q_ref[0]
        lane = lax.broadcasted_iota(I32, q.shape, 1)
        zero = jnp.zeros_like(q)
        qs[0:tq, :] = jnp.where(lane < DA_QK_DIM, q, zero)
        qs[tq:2 * tq, :] = jnp.where(lane >= DA_QK_DIM, q, zero)
        m_s[...] = jnp.full(m_s.shape, NEG_INF, F32)
        acc[...] = jnp.zeros(acc.shape, F32)

    def update(diag):
        k = k_ref[0]
        v_aug = jnp.concatenate([v_ref[0], jnp.ones((tk, LANES), BF16)], axis=1)
        groups = [(gi, slice(mp * tq + gi * tk, mp * tq + (gi + 1) * tk))
                  for gi in range(ng) if gi >= diag for mp in range(2)]
        ss = [lax.dot_general(qs[g, :], k, (((1,), (1,)), ((), ())), preferred_element_type=F32)
              for _, g in groups]
        if diag >= 0:
            row = lax.broadcasted_iota(I32, (tk, tk), 0)
            col = lax.broadcasted_iota(I32, (tk, tk), 1)
            ss = [jnp.where(row >= col, s, NEG_INF) if gi == diag else s
                  for (gi, _), s in zip(groups, ss)]
        for (_, g), s in zip(groups, ss):
            m_old = m_s[g, :]
            m_new = jnp.maximum(m_old, jnp.max(s, axis=-1, keepdims=True))
            p = jnp.exp2(s - jnp.tile(m_new, (1, reps)))
            alpha = jnp.exp2(m_old - m_new)
            pv = jnp.dot(p.astype(BF16), v_aug, preferred_element_type=F32)
            acc[g, :] = jnp.tile(alpha, (1, 2)) * acc[g, :] + pv
            m_s[g, :] = m_new

    @pl.when(ki < qi * ng)
    def _():
        update(-1)

    for d in range(ng):
        @pl.when(ki == qi * ng + d)
        def _(d=d):
            update(d)

    @pl.when(ki == qi * ng + ng - 1)
    def _():
        lamv = lamv_ref[...]
        lam = (jnp.exp(jnp.sum(lamv[0:1] * lamv[1:2], axis=-1, keepdims=True))
               - jnp.exp(jnp.sum(lamv[2:3] * lamv[3:4], axis=-1, keepdims=True)) + lam_init)
        on = acc[:, 0:LANES] / acc[:, LANES:2 * LANES]
        o = on[0:tq] - lam * on[tq:2 * tq]
        ms = jnp.mean(o * o, axis=-1, keepdims=True)
        o = o * lax.rsqrt(ms + RMS_EPS) * g_ref[...] * (1.0 - lam_init)
        o_ref[0] = o.astype(o_ref.dtype)


def _diff_attention(qkv, lamv, norm_g, lam_init, tq, tk):
    b, s, _ = qkv.shape
    ng = tq // tk
    pairs = [(qi, ki) for qi in range(s // tq) for ki in range((qi + 1) * ng)]
    qi_tab = jnp.asarray([p[0] for p in pairs], I32)
    ki_tab = jnp.asarray([p[1] for p in pairs], I32)
    kern = functools.partial(_attn_kernel, lam_init=lam_init, tq=tq, tk=tk)
    grid_spec = pltpu.PrefetchScalarGridSpec(
        num_scalar_prefetch=2,
        grid=(b, DA_HEADS, len(pairs)),
        in_specs=[
            pl.BlockSpec((1, tq, LANES), lambda bi, h, t, qt, kt: (bi, qt[t], h)),
            pl.BlockSpec((1, tk, LANES), lambda bi, h, t, qt, kt: (bi, kt[t], DA_HEADS + h)),
            pl.BlockSpec((1, tk, LANES), lambda bi, h, t, qt, kt: (bi, kt[t], 2 * DA_HEADS + h)),
            pl.BlockSpec((8, LANES), lambda bi, h, t, qt, kt: (0, 0)),
            pl.BlockSpec((1, LANES), lambda bi, h, t, qt, kt: (0, 0)),
        ],
        out_specs=pl.BlockSpec((1, tq, LANES), lambda bi, h, t, qt, kt: (bi, qt[t], h)),
        scratch_shapes=[pltpu.VMEM((2 * tq, LANES), BF16), pltpu.VMEM((2 * tq, LANES), F32),
                        pltpu.VMEM((2 * tq, 2 * LANES), F32)],
    )
    return pl.pallas_call(
        kern,
        grid_spec=grid_spec,
        out_shape=jax.ShapeDtypeStruct((b, s, SEC), BF16),
        compiler_params=_cparams(("parallel", "parallel", "arbitrary")),
        name="diff_attn",
    )(qi_tab, ki_tab, qkv, qkv, qkv, lamv, norm_g)


def _log_sigmoid(x):
    return jnp.minimum(x, 0.0) - jnp.log1p(jnp.exp(-jnp.abs(x)))


def _sigmoid(x):
    return 1.0 / (1.0 + jnp.exp(-x))


def _mlstm_kernel(ml_ref, gt_ref, cw_ref, cb_ref, gb_ref, o_ref, xbuf, cst, mst, *, chunk, group):
    @pl.when(pl.program_id(1) == 0)
    def _():
        xbuf[:, 0:ML_HALO, :] = jnp.zeros((group, ML_HALO, SEC), F32)
        cst[...] = jnp.zeros(cst.shape, F32)
        mst[...] = jnp.full(mst.shape, NEG_INF, F32)

    for bb in range(group):
        _mlstm_chunk(bb, ml_ref, gt_ref, cw_ref, cb_ref, gb_ref, o_ref, xbuf, cst, mst, chunk)


ML_HALO = 8


def _mlstm_chunk(bb, ml_ref, gt_ref, cw_ref, cb_ref, gb_ref, o_ref, xbuf, cst, mst, chunk):
    L = chunk
    HALO = ML_HALO

    xbuf[bb, HALO:HALO + L, :] = ml_ref[bb, :, 0:SEC]
    y = cb_ref[...] + cw_ref[CONV_WIDTH - 1:CONV_WIDTH, :] * xbuf[bb, HALO:HALO + L, :]
    for j in range(CONV_WIDTH - 1):
        off = HALO - (CONV_WIDTH - 1) + j
        y = y + cw_ref[j:j + 1, :] * xbuf[bb, off:off + L, :]
    xbuf[bb, 0:HALO, :] = xbuf[bb, L:L + HALO, :]
    qk = y * _sigmoid(y)

    gi = gt_ref[bb, :, 0:LANES] + gb_ref[:, 0:LANES]
    lf = _log_sigmoid(gt_ref[bb, :, LANES:GATE_W] + gb_ref[:, LANES:GATE_W])
    row = lax.broadcasted_iota(I32, (L, L), 0)
    col = lax.broadcasted_iota(I32, (L, L), 1)
    causal = row >= col
    bcum = jnp.dot(causal.astype(F32), lf, preferred_element_type=F32,
                   precision=lax.Precision.HIGHEST)
    g = bcum[L - 1:L, :]
    w_end = g - bcum + gi
    m_loc = jnp.max(w_end, axis=0, keepdims=True)
    e_end = jnp.exp(w_end - m_loc)
    m_prev = mst[bb]
    m_new = jnp.maximum(g + m_prev, m_loc)
    a_dec = jnp.exp(g + m_prev - m_new)
    b_dec = jnp.exp(m_loc - m_new)
    mst[bb] = m_new
    inter_log = bcum + m_prev
    bcum_t = bcum.T
    gi_t = gi.T

    lane = lax.broadcasted_iota(I32, (L, LANES), 1)
    sub = lax.broadcasted_iota(I32, (LANES, 1), 0)
    ones_col = jnp.where(lane == 0, 1.0, 0.0).astype(BF16)
    for p in range(ML_HEADS // 2):
        q_pair = qk[:, p * LANES:(p + 1) * LANES] * (ML_QK_DIM ** -0.5)
        k_pair = qk[:, 2 * LANES + p * LANES:2 * LANES + (p + 1) * LANES]
        kb = k_pair.astype(BF16)
        cp = bb * (ML_HEADS // 2) + p
        c_prev = cst[cp].astype(BF16)
        upd = jnp.zeros((LANES, 2 * LANES), F32)
        for hh in range(2):
            h = 2 * p + hh
            head_lanes = (lane >= hh * ML_QK_DIM) & (lane < (hh + 1) * ML_QK_DIM)
            qm = jnp.where(head_lanes, q_pair, 0.0).astype(BF16)
            s = lax.dot_general(qm, kb, (((1,), (1,)), ((), ())), preferred_element_type=F32)
            d = jnp.where(causal, bcum[:, h:h + 1] - bcum_t[h:h + 1, :] + gi_t[h:h + 1, :], NEG_INF)
            il = inter_log[:, h:h + 1]
            m_t = jnp.maximum(il, jnp.max(d, axis=-1, keepdims=True))
            sd = (s * jnp.exp(d - m_t)).astype(BF16)
            inter_w = jnp.exp(il - m_t)
            v_aug = jnp.concatenate(
                [ml_ref[bb, :, SEC + h * LANES:SEC + (h + 1) * LANES].astype(BF16), ones_col], axis=1)
            intra = jnp.dot(sd, v_aug, preferred_element_type=F32)
            inter = jnp.dot(qm, c_prev, preferred_element_type=F32)
            num = inter_w * inter[:, 0:LANES] + intra[:, 0:LANES]
            den = inter_w * inter[:, LANES:LANES + 1] + intra[:, LANES:LANES + 1]
            hid = num / jnp.maximum(jnp.abs(den), jnp.exp(-m_t))
            o_gate = ml_ref[bb, :, 2 * SEC + h * LANES:2 * SEC + (h + 1) * LANES]
            o_ref[bb, :, h * LANES:(h + 1) * LANES] = (hid * _sigmoid(o_gate)).astype(o_ref.dtype)
            ek = jnp.where(head_lanes, e_end[:, h:h + 1] * k_pair, 0.0).astype(BF16)
            upd = upd + lax.dot_general(ek, v_aug, (((0,), (0,)), ((), ())),
                                        preferred_element_type=F32)
        first = sub < ML_QK_DIM
        a_rows = jnp.where(first, a_dec[:, 2 * p:2 * p + 1], a_dec[:, 2 * p + 1:2 * p + 2])
        b_rows = jnp.where(first, b_dec[:, 2 * p:2 * p + 1], b_dec[:, 2 * p + 1:2 * p + 2])
        cst[cp] = a_rows * cst[cp] + b_rows * upd


def _mlstm(ml, gates, conv_w, conv_b, gate_b, chunk, group):
    b, s, _ = ml.shape
    kern = functools.partial(_mlstm_kernel, chunk=chunk, group=group)
    return pl.pallas_call(
        kern,
        grid=(b // group, s // chunk),
        in_specs=[pl.BlockSpec((group, chunk, 3 * SEC), lambda bi, c: (bi, c, 0)),
                  pl.BlockSpec((group, chunk, GATE_W), lambda bi, c: (bi, c, 0)),
                  pl.BlockSpec((CONV_WIDTH, SEC), lambda bi, c: (0, 0)),
                  pl.BlockSpec((1, SEC), lambda bi, c: (0, 0)),
                  pl.BlockSpec((1, GATE_W), lambda bi, c: (0, 0))],
        out_specs=pl.BlockSpec((group, chunk, SEC), lambda bi, c: (bi, c, 0)),
        out_shape=jax.ShapeDtypeStruct((b, s, SEC), BF16),
        scratch_shapes=[pltpu.VMEM((group, chunk + 8, SEC), F32),
                        pltpu.VMEM((group * (ML_HEADS // 2), LANES, 2 * LANES), F32),
                        pltpu.VMEM((group, 1, LANES), F32)],
        compiler_params=_cparams(("parallel", "arbitrary")),
        name="mlstm",
    )(ml, gates, conv_w, conv_b, gate_b)


SUBL = D_MODEL // LANES


def _to_tiles(ref, val, base=0):
    rows = val.shape[0]
    for s in range(SUBL):
        ref[pl.ds(base * SUBL + s, rows, stride=SUBL), :] = val[:, s * LANES:(s + 1) * LANES]


def _tile(ref, t):
    return ref.at[pl.ds(pl.multiple_of(t * SUBL, SUBL), SUBL), :]


def _from_tiles(ref, base, rows):
    return jnp.concatenate(
        [ref[pl.ds(base * SUBL + s, rows, stride=SUBL), :] for s in range(SUBL)], axis=1)


def _layer_norm(h, g, b):
    mu = jnp.mean(h, axis=-1, keepdims=True)
    hc = h - mu
    var = jnp.mean(hc * hc, axis=-1, keepdims=True)
    return hc * lax.rsqrt(var + LN_EPS) * g + b


def _outproj_router_kernel(da_ref, mlo_ref, x_ref, wo_ref, g_ref, b_ref, wr_ref, br_ref,
                           x1_ref, te_ref, tg_ref, rk_ref, cnt_ref, bc_ref, carry, *, tm):
    i = pl.program_id(0)

    @pl.when(i == 0)
    def _():
        carry[...] = jnp.zeros(carry.shape, F32)

    mix = (jnp.dot(da_ref[...], wo_ref[0:SEC, :], preferred_element_type=F32)
           + jnp.dot(mlo_ref[...], wo_ref[SEC:2 * SEC, :], preferred_element_type=F32))
    x1 = _layer_norm(DN_ALPHA * x_ref[...] + mix, g_ref[...], b_ref[...])
    x1_ref[...] = x1

    x_hi = x1.astype(BF16)
    x_lo = (x1 - x_hi.astype(F32)).astype(BF16)
    hh_hl = jnp.dot(x_hi, wr_ref[...], preferred_element_type=F32)
    lh = jnp.dot(x_lo, wr_ref[:, 0:LANES], preferred_element_type=F32)
    logits = hh_hl[:, 0:LANES] + (hh_hl[:, LANES:2 * LANES] + lh) + br_ref[...]
    lane = lax.broadcasted_iota(I32, (tm, LANES), 1)
    lane_f = lane.astype(F32)
    work = logits
    hot = []
    vals = []
    idxs = []
    for _k in range(TOP_K):
        mx = jnp.max(work, axis=-1, keepdims=True)
        idx = jnp.min(jnp.where(work == mx, lane_f, float(LANES)), axis=-1, keepdims=True)
        sel = lane_f == idx
        hot.append(sel)
        vals.append(mx)
        idxs.append(idx)
        work = jnp.where(sel, NEG_INF, work)
    exps = [jnp.exp(v - vals[0]) for v in vals]
    inv = 1.0 / (exps[0] + exps[1] + exps[2] + exps[3])

    onehot = (hot[0] | hot[1] | hot[2] | hot[3])
    oh = jnp.where(onehot, 1.0, 0.0)
    row = lax.broadcasted_iota(I32, (tm, tm), 0)
    col = lax.broadcasted_iota(I32, (tm, tm), 1)
    before = jnp.dot((row > col).astype(BF16), oh.astype(BF16), preferred_element_type=F32)
    pos = before + carry[...]
    te = jnp.zeros((tm, LANES), I32)
    tg = jnp.zeros((tm, LANES), F32)
    rk = jnp.zeros((tm, LANES), I32)
    for k in range(TOP_K):
        slot = lane == k
        r_k = jnp.sum(jnp.where(hot[k], pos, 0.0), axis=-1, keepdims=True)
        te = jnp.where(slot, idxs[k].astype(I32), te)
        tg = jnp.where(slot, exps[k] * inv, tg)
        rk = jnp.where(slot, r_k.astype(I32), rk)
    te_ref[...] = te
    tg_ref[...] = tg
    rk_ref[...] = rk
    in_block = jnp.sum(oh, axis=0, keepdims=True)
    bc_ref[...] = jnp.broadcast_to(in_block, bc_ref.shape).astype(I32)
    carry[...] = carry[...] + in_block
    cnt_ref[...] = jnp.broadcast_to(carry[...], cnt_ref.shape).astype(I32)


def _outproj_router(da, mlo, x, wo, ln_g, ln_b, wr, br, tm):
    n = x.shape[0]
    kern = functools.partial(_outproj_router_kernel, tm=tm)
    row_blk = lambda w: pl.BlockSpec((tm, w), lambda i: (i, 0))
    const = lambda r, w: pl.BlockSpec((r, w), lambda i: (0, 0))
    return pl.pallas_call(
        kern,
        grid=(n // tm,),
        in_specs=[row_blk(SEC), row_blk(SEC), row_blk(D_MODEL), const(2 * SEC, D_MODEL),
                  const(1, D_MODEL), const(1, D_MODEL), const(D_MODEL, 2 * LANES), const(1, LANES)],
        out_specs=[row_blk(D_MODEL), row_blk(LANES), row_blk(LANES), row_blk(LANES), const(8, LANES),
                   pl.BlockSpec((8, LANES), lambda i: (i, 0))],
        out_shape=[jax.ShapeDtypeStruct((n, D_MODEL), F32),
                   jax.ShapeDtypeStruct((n, LANES), I32),
                   jax.ShapeDtypeStruct((n, LANES), F32),
                   jax.ShapeDtypeStruct((n, LANES), I32),
                   jax.ShapeDtypeStruct((8, LANES), I32),
                   jax.ShapeDtypeStruct((n // tm * 8, LANES), I32)],
        scratch_shapes=[pltpu.VMEM((1, LANES), F32)],
        compiler_params=_cparams(("arbitrary",)),
        name="outproj_router",
    )(da, mlo, x, wo, ln_g, ln_b, wr, br)


def _dispatch_kernel(base_ref, cnt_ref, lo_ref, hi_ref, x_ref, te_ref, xs_ref, xsort, zeros, sem, zsem,
                     *, tm):
    b = pl.program_id(0)
    pairs = tm * TOP_K
    slot = b % 2

    @pl.when(b == 0)
    def _():
        zeros[...] = jnp.zeros(zeros.shape, F32)

        def per_expert(e, carry):
            def fill(r, c):
                pltpu.make_async_copy(zeros, _tile(xs_ref, r), zsem).start()
                return c

            def drain(r, c):
                pltpu.make_async_copy(zeros, _tile(xs_ref, r), zsem).wait()
                return c

            lax.fori_loop(lo_ref[e], hi_ref[e], fill, 0)
            lax.fori_loop(lo_ref[e], hi_ref[e], drain, 0)
            return carry

        lax.fori_loop(0, N_EXPERTS, per_expert, 0)

    te = te_ref[...]
    lane = lax.broadcasted_iota(I32, (tm, LANES), 1)
    hot = [lane == te[:, k:k + 1] for k in range(TOP_K)]
    oh = jnp.where(hot[0] | hot[1] | hot[2] | hot[3], 1.0, 0.0)
    row = lax.broadcasted_iota(I32, (tm, tm), 0)
    col = lax.broadcasted_iota(I32, (tm, tm), 1)
    before = jnp.dot((row > col).astype(BF16), oh.astype(BF16), preferred_element_type=F32)
    lane1 = lax.broadcasted_iota(I32, (1, LANES), 1)
    offs = jnp.zeros((1, LANES), F32)
    starts = []
    run = jnp.int32(0)
    for e in range(N_EXPERTS):
        starts.append(run)
        offs = jnp.where(lane1 == e, run.astype(F32), offs)
        run = run + cnt_ref[b * N_EXPERTS + e]
    pos = before + offs
    where_to = jnp.zeros((tm, LANES), F32)
    for k in range(TOP_K):
        j_k = jnp.sum(jnp.where(hot[k], pos, 0.0), axis=-1, keepdims=True)
        where_to = jnp.where(lane == k, j_k, where_to)
    slots = where_to.T.astype(I32)
    rid = lax.broadcasted_iota(I32, (pairs, tm), 0)
    pick = rid == slots[0:1, :]
    for k in range(1, TOP_K):
        pick = pick | (rid == slots[k:k + 1, :])
    perm = jnp.where(pick, 1.0, 0.0).astype(BF16)
    sorted_rows = jnp.dot(perm, x_ref[...].astype(BF16), preferred_element_type=F32)
    _to_tiles(xsort.at[slot], sorted_rows)

    sizes = [1 << s for s in range(tm.bit_length() - 1, -1, -1)]
    for e in range(N_EXPERTS):
        c = cnt_ref[b * N_EXPERTS + e]
        dst = base_ref[b * N_EXPERTS + e]
        done = jnp.int32(0)
        for sz in sizes:
            @pl.when((c & sz) != 0)
            def _(sz=sz, done=done, e=e, dst=dst):
                src0 = pl.multiple_of((starts[e] + done) * SUBL, SUBL)
                dst0 = pl.multiple_of((dst + done) * SUBL, SUBL)
                pltpu.make_async_copy(xsort.at[slot, pl.ds(src0, sz * SUBL), :],
                                      xs_ref.at[pl.ds(dst0, sz * SUBL), :], sem.at[slot]).start()
            done = done + (c & sz)
    def drain(which):
        pltpu.make_async_copy(xsort.at[which], xs_ref.at[pl.ds(0, pairs * SUBL), :], sem.at[which]).wait()

    @pl.when(b > 0)
    def _():
        drain(1 - slot)

    @pl.when(b == pl.num_programs(0) - 1)
    def _():
        drain(slot)


def _dispatch(x1, te, base, cnt, seg_lo, seg_hi, n_rows, tm):
    n = x1.shape[0]
    kern = functools.partial(_dispatch_kernel, tm=tm)
    grid_spec = pltpu.PrefetchScalarGridSpec(
        num_scalar_prefetch=4,
        grid=(n // tm,),
        in_specs=[pl.BlockSpec((tm, D_MODEL), lambda i, *_: (i, 0)),
                  pl.BlockSpec((tm, LANES), lambda i, *_: (i, 0))],
        out_specs=pl.BlockSpec(memory_space=pl.ANY),
        scratch_shapes=[pltpu.VMEM((2, tm * TOP_K * SUBL, LANES), F32), pltpu.VMEM((SUBL, LANES), F32),
                        pltpu.SemaphoreType.DMA((2,)), pltpu.SemaphoreType.DMA(())],
    )
    return pl.pallas_call(
        kern,
        grid_spec=grid_spec,
        out_shape=jax.ShapeDtypeStruct((n_rows * SUBL, LANES), F32),
        compiler_params=_cparams(("arbitrary",)),
        name="moe_dispatch",
    )(base, cnt, seg_lo, seg_hi, x1, te)


FF_CHUNK = 512


def _expert_kernel(be_ref, nused_ref, nxt_ref, xs_ref, wgu_hbm, bgu_ref, wd_hbm, bd_ref, ys_ref,
                   wgu_b, wd_b, wgu_f, wd_f, wsem, *, tm, layer):
    i = pl.program_id(0)
    live = i < nused_ref[0]
    fresh = jnp.logical_or(i == 0, be_ref[i] != be_ref[jnp.maximum(i - 1, 0)])

    def weight_copies(e):
        w = layer * N_EXPERTS + e
        return (pltpu.make_async_copy(wgu_hbm.at[w], wgu_f, wsem.at[0]),
                pltpu.make_async_copy(wd_hbm.at[w], wd_f, wsem.at[1]))

    @pl.when(i == 0)
    def _():
        for cp in weight_copies(be_ref[0]):
            cp.start()

    @pl.when(jnp.logical_and(live, fresh))
    def _():
        for cp in weight_copies(be_ref[i]):
            cp.wait()
        wgu_b[...] = wgu_f[...].astype(BF16)
        wd_b[...] = wd_f[...].astype(BF16)

        @pl.when(nxt_ref[i] >= 0)
        def _():
            for cp in weight_copies(nxt_ref[i]):
                cp.start()

    @pl.when(live)
    def _():
        x16 = _from_tiles(xs_ref, 0, tm).astype(BF16)
        y = None
        for c in range(D_FF // FF_CHUNK):
            lo, hi = c * FF_CHUNK, (c + 1) * FF_CHUNK
            hg = jnp.dot(x16, wgu_b[:, lo:hi], preferred_element_type=F32) + bgu_ref[0, :, lo:hi]
            hu = (jnp.dot(x16, wgu_b[:, D_FF + lo:D_FF + hi], preferred_element_type=F32)
                  + bgu_ref[0, :, D_FF + lo:D_FF + hi])
            gate = jnp.minimum(hg, SWIGLU_LIMIT)
            up = jnp.clip(hu, -SWIGLU_LIMIT, SWIGLU_LIMIT)
            act = ((up + 1.0) * (gate * _sigmoid(SWIGLU_ALPHA * gate))).astype(BF16)
            part = jnp.dot(act, wd_b[lo:hi, :], preferred_element_type=F32)
            y = part if y is None else y + part
        _to_tiles(ys_ref, y + bd_ref[0])

    @pl.when(jnp.logical_not(live))
    def _():
        ys_ref[...] = jnp.zeros(ys_ref.shape, F32)


def _experts(xs, block_e, nused, next_e, wgu, bgu, wd, bd, layer, tm):
    n_rows = xs.shape[0] // SUBL
    nb = n_rows // tm

    def rows(i, be, nu, nx):
        return (jnp.minimum(i, nu[0] - 1), 0)

    def per_e(i, be, nu, nx):
        return (layer * N_EXPERTS + be[i], 0, 0)

    grid_spec = pltpu.PrefetchScalarGridSpec(
        num_scalar_prefetch=3,
        grid=(nb,),
        in_specs=[pl.BlockSpec((tm * SUBL, LANES), rows),
                  pl.BlockSpec(memory_space=pl.ANY),
                  pl.BlockSpec((1, 1, 2 * D_FF), per_e),
                  pl.BlockSpec(memory_space=pl.ANY),
                  pl.BlockSpec((1, 1, D_MODEL), per_e)],
        out_specs=pl.BlockSpec((tm * SUBL, LANES), lambda i, be, nu, nx: (i, 0)),
        scratch_shapes=[pltpu.VMEM((D_MODEL, 2 * D_FF), BF16), pltpu.VMEM((D_FF, D_MODEL), BF16),
                        pltpu.VMEM((D_MODEL, 2 * D_FF), F32), pltpu.VMEM((D_FF, D_MODEL), F32),
                        pltpu.SemaphoreType.DMA((2,))],
    )
    return pl.pallas_call(
        functools.partial(_expert_kernel, tm=tm, layer=layer),
        grid_spec=grid_spec,
        out_shape=jax.ShapeDtypeStruct((n_rows * SUBL, LANES), F32),
        compiler_params=_cparams(("arbitrary",)),
        name="moe_experts",
    )(block_e, nused, next_e, xs, wgu, bgu, wd, bd)


def _combine_kernel(dcur_ref, dnext_ref, ys_ref, x1_ref, tg_ref, g_ref, b_ref, o_ref, buf, sem, *, tm):
    i = pl.program_id(0)
    last = pl.num_programs(0) - 1
    p = i % 2
    rows = TOP_K * tm * SUBL

    def fetch(dest_ref, slot):
        def issue(t, carry):
            for k in range(TOP_K):
                pltpu.make_async_copy(_tile(ys_ref, dest_ref[t * TOP_K + k]),
                                      _tile(buf.at[slot], k * tm + t), sem.at[slot]).start(priority=k % 2)
            return carry

        lax.fori_loop(0, tm, issue, 0)

    @pl.when(i == 0)
    def _():
        fetch(dcur_ref, 0)

    @pl.when(i < last)
    def _():
        fetch(dnext_ref, 1 - p)

    pltpu.make_async_copy(ys_ref.at[pl.ds(0, rows), :], buf.at[p], sem.at[p]).wait()
    tg = tg_ref[...]
    cur = buf.at[p]
    y = tg[:, 0:1] * _from_tiles(cur, 0, tm)
    for k in range(1, TOP_K):
        y = y + tg[:, k:k + 1] * _from_tiles(cur, k * tm, tm)
    o_ref[...] = _layer_norm(DN_ALPHA * x1_ref[...] + y, g_ref[...], b_ref[...])


def _combine(ys, dest_flat, x1, tg, ln_g, ln_b, tm):
    n = x1.shape[0]
    nsteps = n // tm
    kern = functools.partial(_combine_kernel, tm=tm)
    return pl.pallas_call(
        kern,
        grid=(nsteps,),
        in_specs=[pl.BlockSpec((tm * TOP_K,), lambda i: (i,), memory_space=pltpu.SMEM),
                  pl.BlockSpec((tm * TOP_K,), lambda i: (jnp.minimum(i + 1, nsteps - 1),),
                               memory_space=pltpu.SMEM),
                  pl.BlockSpec(memory_space=pl.ANY),
                  pl.BlockSpec((tm, D_MODEL), lambda i: (i, 0)),
                  pl.BlockSpec((tm, LANES), lambda i: (i, 0)),
                  pl.BlockSpec((1, D_MODEL), lambda i: (0, 0)),
                  pl.BlockSpec((1, D_MODEL), lambda i: (0, 0))],
        out_specs=pl.BlockSpec((tm, D_MODEL), lambda i: (i, 0)),
        out_shape=jax.ShapeDtypeStruct((n, D_MODEL), F32),
        scratch_shapes=[pltpu.VMEM((2, TOP_K * tm * SUBL, LANES), F32), pltpu.SemaphoreType.DMA((2,))],
        compiler_params=_cparams(("arbitrary",)),
        name="moe_combine",
    )(dest_flat, dest_flat, ys, x1, tg, ln_g, ln_b)


def _pad_lanes(v, width=LANES, value=0.0):
    return jnp.pad(v, ((0, 0), (0, width - v.shape[-1])), constant_values=value)


def _prep_w_in(w):
    body = w[:, :6 * SEC]
    gi = _pad_lanes(w[:, 6 * SEC:6 * SEC + ML_HEADS])
    gf = _pad_lanes(w[:, 6 * SEC + ML_HEADS:6 * SEC + 2 * ML_HEADS])
    return jnp.concatenate([body, gi, gf], axis=1).astype(BF16)


def _layer(x, p, big, layer, lam_init, cfg):
    n = x.shape[0]
    b, s = cfg["batch"], cfg["seq"]
    tm_e = cfg["tm_expert"]

    qkv, ml, gates = _inproj(x, _prep_w_in(p["w_in"]), cfg["tm_proj"])

    lamv = jnp.zeros((8, LANES), F32)
    for r, nm in enumerate(("lam_q1", "lam_k1", "lam_q2", "lam_k2")):
        lamv = lamv.at[r, :DA_QK_DIM].set(p[nm])
    da = _diff_attention(qkv.reshape(b, s, 3 * SEC), lamv, p["da_norm_g"].reshape(1, LANES),
                         lam_init, cfg["tq"], cfg["tk"])

    gate_b = jnp.concatenate([_pad_lanes(p["gate_b"][None, :ML_HEADS]),
                              _pad_lanes(p["gate_b"][None, ML_HEADS:])], axis=1)
    mlo = _mlstm(ml.reshape(b, s, 3 * SEC), gates.reshape(b, s, GATE_W), p["conv_w"],
                 p["conv_b"].reshape(1, SEC), gate_b, cfg["chunk"], cfg["ml_group"])

    wr = _pad_lanes(p["w_router"])
    wr_hi = wr.astype(BF16)
    wr = jnp.concatenate([wr_hi, (wr - wr_hi.astype(F32)).astype(BF16)], axis=1)
    br = _pad_lanes(p["b_router"][None, :], value=NEG_INF)
    x1, te, tg, rk, cnt, bcnt = _outproj_router(
        da.reshape(n, SEC), mlo.reshape(n, SEC), x, p["w_out"].astype(BF16),
        p["ln1_g"].reshape(1, D_MODEL), p["ln1_b"].reshape(1, D_MODEL), wr, br, cfg["tm_route"])

    counts = cnt[0, :N_EXPERTS]
    padded = (counts + tm_e - 1) // tm_e * tm_e
    pad_ends = jnp.cumsum(padded)
    pad_starts = pad_ends - padded
    top_e = te[:, :TOP_K]
    sel = top_e[:, :, None] == jnp.arange(N_EXPERTS, dtype=I32)[None, None, :]
    dest = rk[:, :TOP_K] + jnp.sum(jnp.where(sel, pad_starts[None, None, :], 0), axis=-1)
    dest_flat = dest.reshape(n * TOP_K).astype(I32)
    n_rows = n * TOP_K + N_EXPERTS * tm_e
    nb = n_rows // tm_e
    blk_start = jnp.arange(nb, dtype=I32) * tm_e
    block_e = jnp.minimum(jnp.sum(blk_start[:, None] >= pad_ends[None, :], axis=-1),
                          N_EXPERTS - 1).astype(I32)
    nused = (pad_ends[-1:] // tm_e).astype(I32)
    ids = jnp.where(padded > 0, jnp.arange(N_EXPERTS, dtype=I32), N_EXPERTS)
    later = jnp.concatenate([lax.cummin(ids[::-1])[::-1][1:], jnp.full((1,), N_EXPERTS, I32)])
    next_e = jnp.where(later >= N_EXPERTS, -1, later)[block_e].astype(I32)
    seg_lo = (pad_starts + counts).astype(I32)
    seg_hi = pad_ends.at[N_EXPERTS - 1].set(n_rows).astype(I32)

    blk_cnt = bcnt[::8, :N_EXPERTS]
    blk_base = pad_starts[None, :] + jnp.cumsum(blk_cnt, axis=0) - blk_cnt
    xs = _dispatch(x1, te, blk_base.reshape(-1).astype(I32), blk_cnt.reshape(-1).astype(I32),
                   seg_lo, seg_hi, n_rows, cfg["tm_route"])
    ys = _experts(xs, block_e, nused, next_e, big["w_gu"], big["b_gu"], big["w_down"],
                  big["b_down"], layer, tm_e)
    return _combine(ys, dest_flat, x1, tg, p["ln2_g"].reshape(1, D_MODEL),
                    p["ln2_b"].reshape(1, D_MODEL), cfg["tm_comb"])


def _forward(x, params, cfg):
    b, s, d = x.shape
    h = x.reshape(b * s, d)
    depth = params["w_in"].shape[0]
    big = dict(w_gu=params["w_gu"].reshape(depth * N_EXPERTS, D_MODEL, 2 * D_FF),
               b_gu=params["b_gu"].reshape(depth * N_EXPERTS, 1, 2 * D_FF),
               w_down=params["w_down"].reshape(depth * N_EXPERTS, D_FF, D_MODEL),
               b_down=params["b_down"].reshape(depth * N_EXPERTS, 1, D_MODEL))
    for l in range(depth):
        lam_init = 0.8 - 0.6 * math.exp(-0.3 * l)
        p = {k: v[l] for k, v in params.items() if k not in big}
        h = _layer(h, p, big, l, lam_init, cfg)
    return h.reshape(b, s, d)


def kernel(x, w_in, conv_w, conv_b, gate_b, lam_q1, lam_k1, lam_q2, lam_k2, da_norm_g, w_out,
           ln1_g, ln1_b, w_router, b_router, w_gu, b_gu, w_down, b_down, ln2_g, ln2_b):
    params = dict(w_in=w_in, conv_w=conv_w, conv_b=conv_b, gate_b=gate_b, lam_q1=lam_q1,
                  lam_k1=lam_k1, lam_q2=lam_q2, lam_k2=lam_k2, da_norm_g=da_norm_g, w_out=w_out,
                  ln1_g=ln1_g, ln1_b=ln1_b, w_router=w_router, b_router=b_router, w_gu=w_gu,
                  b_gu=b_gu, w_down=w_down, b_down=b_down, ln2_g=ln2_g, ln2_b=ln2_b)
    cfg = dict(batch=x.shape[0], seq=x.shape[1], tm_proj=512, tm_route=512, tq=2048, tk=512, chunk=256, ml_group=4,
               tm_expert=512, tm_comb=256)
    return _forward(x, params, cfg)
```

```python
import functools
import math

import jax
import jax.numpy as jnp
from jax import lax
from jax.experimental import pallas as pl
from jax.experimental.pallas import tpu as pltpu

F32 = jnp.float32
BF16 = jnp.bfloat16
I32 = jnp.int32

D_MODEL = 1024
DEPTH = 4
DA_HEADS = 4
DA_QK_DIM = 64
DA_V_DIM = 128
ML_HEADS = 4
ML_QK_DIM = 64
ML_V_DIM = 128
CONV_WIDTH = 4
N_EXPERTS = 32
TOP_K = 4
D_FF = 1024
SWIGLU_LIMIT = 7.0
SWIGLU_ALPHA = 1.702
DN_ALPHA = (2 * DEPTH) ** 0.25
LN_EPS = 1e-5
RMS_EPS = 1e-5

LANES = 128
SEC = 512
GATE_W = 2 * LANES
IN_W_PAD = 6 * SEC + GATE_W
VMEM_LIMIT = 56 * 1024 * 1024
NEG_INF = float("-inf")
DA_Q_SCALE = DA_QK_DIM ** -0.5 * math.log2(math.e)


def _cparams(sem):
    return pltpu.CompilerParams(dimension_semantics=sem, vmem_limit_bytes=VMEM_LIMIT)


def _inproj_kernel(x_ref, w_ref, qkv_ref, ml_ref, gate_ref):
    xb = x_ref[...].astype(BF16)
    for c in range(3):
        sl = slice(c * SEC, (c + 1) * SEC)
        r = jnp.dot(xb, w_ref[:, sl], preferred_element_type=F32)
        if c == 0:
            r = r * DA_Q_SCALE
        qkv_ref[:, sl] = r.astype(BF16)
    for c in range(3):
        ml_ref[:, c * SEC:(c + 1) * SEC] = jnp.dot(
            xb, w_ref[:, (3 + c) * SEC:(4 + c) * SEC], preferred_element_type=F32)
    gate_ref[...] = jnp.dot(xb, w_ref[:, 6 * SEC:], preferred_element_type=F32)


def _inproj(x, w, tm):
    n = x.shape[0]
    return pl.pallas_call(
        _inproj_kernel,
        grid=(n // tm,),
        in_specs=[pl.BlockSpec((tm, D_MODEL), lambda i: (i, 0)),
                  pl.BlockSpec((D_MODEL, IN_W_PAD), lambda i: (0, 0))],
        out_specs=[pl.BlockSpec((tm, 3 * SEC), lambda i: (i, 0)),
                   pl.BlockSpec((tm, 3 * SEC), lambda i: (i, 0)),
                   pl.BlockSpec((tm, GATE_W), lambda i: (i, 0))],
        out_shape=[jax.ShapeDtypeStruct((n, 3 * SEC), BF16),
                   jax.ShapeDtypeStruct((n, 3 * SEC), F32),
                   jax.ShapeDtypeStruct((n, GATE_W), F32)],
        compiler_params=_cparams(("parallel",)),
        name="in_proj",
    )(x, w)


def _attn_kernel(qi_tab, ki_tab, q_ref, k_ref, v_ref, lamv_ref, g_ref, o_ref, qs, m_s, acc,
                 *, lam_init, tq, tk):
    step = pl.program_id(2)
    qi = qi_tab[step]
    ki = ki_tab[step]
    ng = tq // tk
    reps = tk // LANES

    @pl.when(ki == 0)
    def _():
        q = q_ref[0]
        lane = lax.broadcasted_iota(I32, q.shape, 1)
        zero = jnp.zeros_like(q)
        qs[0:tq, :] = jnp.where(lane < DA_QK_DIM, q, zero)
        qs[tq:2 * tq, :] = jnp.where(lane >= DA_QK_DIM, q, zero)
        m_s[...] = jnp.full(m_s.shape, NEG_INF, F32)
        acc[...] = jnp.zeros(acc.shape, F32)

    def update(diag):
        k = k_ref[0]
        v_aug = jnp.concatenate([v_ref[0], jnp.ones((tk, LANES), BF16)], axis=1)
        groups = [(gi, slice(mp * tq + gi * tk, mp * tq + (gi + 1) * tk))
                  for gi in range(ng) if gi >= diag for mp in range(2)]
        ss = [lax.dot_general(qs[g, :], k, (((1,), (1,)), ((), ())), preferred_element_type=F32)
              for _, g in groups]
        if diag >= 0:
            row = lax.broadcasted_iota(I32, (tk, tk), 0)
            col = lax.broadcasted_iota(I32, (tk, tk), 1)
            ss = [jnp.where(row >= col, s, NEG_INF) if gi == diag else s
                  for (gi, _), s in zip(groups, ss)]
        for (_, g), s in zip(groups, ss):
            m_old = m_s[g, :]
            m_new = jnp.maximum(m_old, jnp.max(s, axis=-1, keepdims=True))
            p = jnp.exp2(s - jnp.tile(m_new, (1, reps)))
            alpha = jnp.exp2(m_old - m_new)
            pv = jnp.dot(p.astype(BF16), v_aug, preferred_element_type=F32)
            acc[g, :] = jnp.tile(alpha, (1, 2)) * acc[g, :] + pv
            m_s[g, :] = m_new

    @pl.when(ki < qi * ng)
    def _():
        update(-1)

    for d in range(ng):
        @pl.when(ki == qi * ng + d)
        def _(d=d):
            update(d)

    @pl.when(ki == qi * ng + ng - 1)
    def _():
        lamv = lamv_ref[...]
        lam = (jnp.exp(jnp.sum(lamv[0:1] * lamv[1:2], axis=-1, keepdims=True))
               - jnp.exp(jnp.sum(lamv[2:3] * lamv[3:4], axis=-1, keepdims=True)) + lam_init)
        on = acc[:, 0:LANES] / acc[:, LANES:2 * LANES]
        o = on[0:tq] - lam * on[tq:2 * tq]
        ms = jnp.mean(o * o, axis=-1, keepdims=True)
        o = o * lax.rsqrt(ms + RMS_EPS) * g_ref[...] * (1.0 - lam_init)
        o_ref[0] = o.astype(o_ref.dtype)


def _diff_attention(qkv, lamv, norm_g, lam_init, tq, tk):
    b, s, _ = qkv.shape
    ng = tq // tk
    pairs = [(qi, ki) for qi in range(s // tq) for ki in range((qi + 1) * ng)]
    qi_tab = jnp.asarray([p[0] for p in pairs], I32)
    ki_tab = jnp.asarray([p[1] for p in pairs], I32)
    kern = functools.partial(_attn_kernel, lam_init=lam_init, tq=tq, tk=tk)
    grid_spec = pltpu.PrefetchScalarGridSpec(
        num_scalar_prefetch=2,
        grid=(b, DA_HEADS, len(pairs)),
        in_specs=[
            pl.BlockSpec((1, tq, LANES), lambda bi, h, t, qt, kt: (bi, qt[t], h)),
            pl.BlockSpec((1, tk, LANES), lambda bi, h, t, qt, kt: (bi, kt[t], DA_HEADS + h)),
            pl.BlockSpec((1, tk, LANES), lambda bi, h, t, qt, kt: (bi, kt[t], 2 * DA_HEADS + h)),
            pl.BlockSpec((8, LANES), lambda bi, h, t, qt, kt: (0, 0)),
            pl.BlockSpec((1, LANES), lambda bi, h, t, qt, kt: (0, 0)),
        ],
        out_specs=pl.BlockSpec((1, tq, LANES), lambda bi, h, t, qt, kt: (bi, qt[t], h)),
        scratch_shapes=[pltpu.VMEM((2 * tq, LANES), BF16), pltpu.VMEM((2 * tq, LANES), F32),
                        pltpu.VMEM((2 * tq, 2 * LANES), F32)],
    )
    return pl.pallas_call(
        kern,
        grid_spec=grid_spec,
        out_shape=jax.ShapeDtypeStruct((b, s, SEC), BF16),
        compiler_params=_cparams(("parallel", "parallel", "arbitrary")),
        name="diff_attn",
    )(qi_tab, ki_tab, qkv, qkv, qkv, lamv, norm_g)


def _log_sigmoid(x):
    return jnp.minimum(x, 0.0) - jnp.log1p(jnp.exp(-jnp.abs(x)))


def _sigmoid(x):
    return 1.0 / (1.0 + jnp.exp(-x))


def _mlstm_kernel(ml_ref, gt_ref, cw_ref, cb_ref, gb_ref, o_ref, xbuf, cst, mst, *, chunk, group):
    @pl.when(pl.program_id(1) == 0)
    def _():
        xbuf[:, 0:ML_HALO, :] = jnp.zeros((group, ML_HALO, SEC), F32)
        cst[...] = jnp.zeros(cst.shape, F32)
        mst[...] = jnp.full(mst.shape, NEG_INF, F32)

    for bb in range(group):
        _mlstm_chunk(bb, ml_ref, gt_ref, cw_ref, cb_ref, gb_ref, o_ref, xbuf, cst, mst, chunk)


ML_HALO = 8


def _mlstm_chunk(bb, ml_ref, gt_ref, cw_ref, cb_ref, gb_ref, o_ref, xbuf, cst, mst, chunk):
    L = chunk
    HALO = ML_HALO

    xbuf[bb, HALO:HALO + L, :] = ml_ref[bb, :, 0:SEC]
    y = cb_ref[...] + cw_ref[CONV_WIDTH - 1:CONV_WIDTH, :] * xbuf[bb, HALO:HALO + L, :]
    for j in range(CONV_WIDTH - 1):
        off = HALO - (CONV_WIDTH - 1) + j
        y = y + cw_ref[j:j + 1, :] * xbuf[bb, off:off + L, :]
    xbuf[bb, 0:HALO, :] = xbuf[bb, L:L + HALO, :]
    qk = y * _sigmoid(y)

    gi = gt_ref[bb, :, 0:LANES] + gb_ref[:, 0:LANES]
    lf = _log_sigmoid(gt_ref[bb, :, LANES:GATE_W] + gb_ref[:, LANES:GATE_W])
    row = lax.broadcasted_iota(I32, (L, L), 0)
    col = lax.broadcasted_iota(I32, (L, L), 1)
    causal = row >= col
    bcum = jnp.dot(causal.astype(F32), lf, preferred_element_type=F32,
                   precision=lax.Precision.HIGHEST)
    g = bcum[L - 1:L, :]
    w_end = g - bcum + gi
    m_loc = jnp.max(w_end, axis=0, keepdims=True)
    e_end = jnp.exp(w_end - m_loc)
    m_prev = mst[bb]
    m_new = jnp.maximum(g + m_prev, m_loc)
    a_dec = jnp.exp(g + m_prev - m_new)
    b_dec = jnp.exp(m_loc - m_new)
    mst[bb] = m_new
    inter_log = bcum + m_prev
    bcum_t = bcum.T
    gi_t = gi.T

    lane = lax.broadcasted_iota(I32, (L, LANES), 1)
    sub = lax.broadcasted_iota(I32, (LANES, 1), 0)
    ones_col = jnp.where(lane == 0, 1.0, 0.0).astype(BF16)
    for p in range(ML_HEADS // 2):
        q_pair = qk[:, p * LANES:(p + 1) * LANES] * (ML_QK_DIM ** -0.5)
        k_pair = qk[:, 2 * LANES + p * LANES:2 * LANES + (p + 1) * LANES]
        kb = k_pair.astype(BF16)
        cp = bb * (ML_HEADS // 2) + p
        c_prev = cst[cp].astype(BF16)
        upd = jnp.zeros((LANES, 2 * LANES), F32)
        for hh in range(2):
            h = 2 * p + hh
            head_lanes = (lane >= hh * ML_QK_DIM) & (lane < (hh + 1) * ML_QK_DIM)
            qm = jnp.where(head_lanes, q_pair, 0.0).astype(BF16)
            s = lax.dot_general(qm, kb, (((1,), (1,)), ((), ())), preferred_element_type=F32)
            d = jnp.where(causal, bcum[:, h:h + 1] - bcum_t[h:h + 1, :] + gi_t[h:h + 1, :], NEG_INF)
            il = inter_log[:, h:h + 1]
            m_t = jnp.maximum(il, jnp.max(d, axis=-1, keepdims=True))
            sd = (s * jnp.exp(d - m_t)).astype(BF16)
            inter_w = jnp.exp(il - m_t)
            v_aug = jnp.concatenate(
                [ml_ref[bb, :, SEC + h * LANES:SEC + (h + 1) * LANES].astype(BF16), ones_col], axis=1)
            intra = jnp.dot(sd, v_aug, preferred_element_type=F32)
            inter = jnp.dot(qm, c_prev, preferred_element_type=F32)
            num = inter_w * inter[:, 0:LANES] + intra[:, 0:LANES]
            den = inter_w * inter[:, LANES:LANES + 1] + intra[:, LANES:LANES + 1]
            hid = num / jnp.maximum(jnp.abs(den), jnp.exp(-m_t))
            o_gate = ml_ref[bb, :, 2 * SEC + h * LANES:2 * SEC + (h + 1) * LANES]
            o_ref[bb, :, h * LANES:(h + 1) * LANES] = (hid * _sigmoid(o_gate)).astype(o_ref.dtype)
            ek = jnp.where(head_lanes, e_end[:, h:h + 1] * k_pair, 0.0).astype(BF16)
            upd = upd + lax.dot_general(ek, v_aug, (((0,), (0,)), ((), ())),
                                        preferred_element_type=F32)
        first = sub < ML_QK_DIM
        a_rows = jnp.where(first, a_dec[:, 2 * p:2 * p + 1], a_dec[:, 2 * p + 1:2 * p + 2])
        b_rows = jnp.where(first, b_dec[:, 2 * p:2 * p + 1], b_dec[:, 2 * p + 1:2 * p + 2])
        cst[cp] = a_rows * cst[cp] + b_rows * upd


def _mlstm(ml, gates, conv_w, conv_b, gate_b, chunk, group):
    b, s, _ = ml.shape
    kern = functools.partial(_mlstm_kernel, chunk=chunk, group=group)
    return pl.pallas_call(
        kern,
        grid=(b // group, s // chunk),
        in_specs=[pl.BlockSpec((group, chunk, 3 * SEC), lambda bi, c: (bi, c, 0)),
                  pl.BlockSpec((group, chunk, GATE_W), lambda bi, c: (bi, c, 0)),
                  pl.BlockSpec((CONV_WIDTH, SEC), lambda bi, c: (0, 0)),
                  pl.BlockSpec((1, SEC), lambda bi, c: (0, 0)),
                  pl.BlockSpec((1, GATE_W), lambda bi, c: (0, 0))],
        out_specs=pl.BlockSpec((group, chunk, SEC), lambda bi, c: (bi, c, 0)),
        out_shape=jax.ShapeDtypeStruct((b, s, SEC), BF16),
        scratch_shapes=[pltpu.VMEM((group, chunk + 8, SEC), F32),
                        pltpu.VMEM((group * (ML_HEADS // 2), LANES, 2 * LANES), F32),
                        pltpu.VMEM((group, 1, LANES), F32)],
        compiler_params=_cparams(("parallel", "arbitrary")),
        name="mlstm",
    )(ml, gates, conv_w, conv_b, gate_b)


SUBL = D_MODEL // LANES


def _to_tiles(ref, val, base=0):
    rows = val.shape[0]
    for s in range(SUBL):
        ref[pl.ds(base * SUBL + s, rows, stride=SUBL), :] = val[:, s * LANES:(s + 1) * LANES]


def _tile(ref, t):
    return ref.at[pl.ds(pl.multiple_of(t * SUBL, SUBL), SUBL), :]


def _from_tiles(ref, base, rows):
    return jnp.concatenate(
        [ref[pl.ds(base * SUBL + s, rows, stride=SUBL), :] for s in range(SUBL)], axis=1)


def _layer_norm(h, g, b):
    mu = jnp.mean(h, axis=-1, keepdims=True)
    hc = h - mu
    var = jnp.mean(hc * hc, axis=-1, keepdims=True)
    return hc * lax.rsqrt(var + LN_EPS) * g + b


def _outproj_router_kernel(da_ref, mlo_ref, x_ref, wo_ref, g_ref, b_ref, wr_ref, br_ref,
                           x1_ref, te_ref, tg_ref, rk_ref, cnt_ref, bc_ref, carry, *, tm):
    i = pl.program_id(0)

    @pl.when(i == 0)
    def _():
        carry[...] = jnp.zeros(carry.shape, F32)

    mix = (jnp.dot(da_ref[...], wo_ref[0:SEC, :], preferred_element_type=F32)
           + jnp.dot(mlo_ref[...], wo_ref[SEC:2 * SEC, :], preferred_element_type=F32))
    x1 = _layer_norm(DN_ALPHA * x_ref[...] + mix, g_ref[...], b_ref[...])
    x1_ref[...] = x1

    x_hi = x1.astype(BF16)
    x_lo = (x1 - x_hi.astype(F32)).astype(BF16)
    hh_hl = jnp.dot(x_hi, wr_ref[...], preferred_element_type=F32)
    lh = jnp.dot(x_lo, wr_ref[:, 0:LANES], preferred_element_type=F32)
    logits = hh_hl[:, 0:LANES] + (hh_hl[:, LANES:2 * LANES] + lh) + br_ref[...]
    lane = lax.broadcasted_iota(I32, (tm, LANES), 1)
    lane_f = lane.astype(F32)
    work = logits
    hot = []
    vals = []
    idxs = []
    for _k in range(TOP_K):
        mx = jnp.max(work, axis=-1, keepdims=True)
        idx = jnp.min(jnp.where(work == mx, lane_f, float(LANES)), axis=-1, keepdims=True)
        sel = lane_f == idx
        hot.append(sel)
        vals.append(mx)
        idxs.append(idx)
        work = jnp.where(sel, NEG_INF, work)
    exps = [jnp.exp(v - vals[0]) for v in vals]
    inv = 1.0 / (exps[0] + exps[1] + exps[2] + exps[3])

    onehot = (hot[0] | hot[1] | hot[2] | hot[3])
    oh = jnp.where(onehot, 1.0, 0.0)
    row = lax.broadcasted_iota(I32, (tm, tm), 0)
    col = lax.broadcasted_iota(I32, (tm, tm), 1)
    before = jnp.dot((row > col).astype(BF16), oh.astype(BF16), preferred_element_type=F32)
    pos = before + carry[...]
    te = jnp.zeros((tm, LANES), I32)
    tg = jnp.zeros((tm, LANES), F32)
    rk = jnp.zeros((tm, LANES), I32)
    for k in range(TOP_K):
        slot = lane == k
        r_k = jnp.sum(jnp.where(hot[k], pos, 0.0), axis=-1, keepdims=True)
        te = jnp.where(slot, idxs[k].astype(I32), te)
        tg = jnp.where(slot, exps[k] * inv, tg)
        rk = jnp.where(slot, r_k.astype(I32), rk)
    te_ref[...] = te
    tg_ref[...] = tg
    rk_ref[...] = rk
    in_block = jnp.sum(oh, axis=0, keepdims=True)
    bc_ref[...] = jnp.broadcast_to(in_block, bc_ref.shape).astype(I32)
    carry[...] = carry[...] + in_block
    cnt_ref[...] = jnp.broadcast_to(carry[...], cnt_ref.shape).astype(I32)


def _outproj_router(da, mlo, x, wo, ln_g, ln_b, wr, br, tm):
    n = x.shape[0]
    kern = functools.partial(_outproj_router_kernel, tm=tm)
    row_blk = lambda w: pl.BlockSpec((tm, w), lambda i: (i, 0))
    const = lambda r, w: pl.BlockSpec((r, w), lambda i: (0, 0))
    return pl.pallas_call(
        kern,
        grid=(n // tm,),
        in_specs=[row_blk(SEC), row_blk(SEC), row_blk(D_MODEL), const(2 * SEC, D_MODEL),
                  const(1, D_MODEL), const(1, D_MODEL), const(D_MODEL, 2 * LANES), const(1, LANES)],
        out_specs=[row_blk(D_MODEL), row_blk(LANES), row_blk(LANES), row_blk(LANES), const(8, LANES),
                   pl.BlockSpec((8, LANES), lambda i: (i, 0))],
        out_shape=[jax.ShapeDtypeStruct((n, D_MODEL), F32),
                   jax.ShapeDtypeStruct((n, LANES), I32),
                   jax.ShapeDtypeStruct((n, LANES), F32),
                   jax.ShapeDtypeStruct((n, LANES), I32),
                   jax.ShapeDtypeStruct((8, LANES), I32),
                   jax.ShapeDtypeStruct((n // tm * 8, LANES), I32)],
        scratch_shapes=[pltpu.VMEM((1, LANES), F32)],
        compiler_params=_cparams(("arbitrary",)),
        name="outproj_router",
    )(da, mlo, x, wo, ln_g, ln_b, wr, br)


def _dispatch_kernel(base_ref, cnt_ref, lo_ref, hi_ref, x_ref, te_ref, xs_ref, xsort, zeros, sem, zsem,
                     *, tm):
    b = pl.program_id(0)
    pairs = tm * TOP_K
    slot = b % 2

    @pl.when(b == 0)
    def _():
        zeros[...] = jnp.zeros(zeros.shape, F32)
        tm_shift = tm.bit_length() - 1

        def pieces(e, act):
            lo = lo_ref[e]
            n = hi_ref[e] - lo
            whole = lax.shift_right_logical(n, tm_shift)

            def full(j, c):
                act(lo + j * tm, tm)
                return c

            lax.fori_loop(0, whole, full, 0)
            done = whole * tm
            for sz in [1 << s for s in range(tm_shift - 1, -1, -1)]:
                @pl.when((n & sz) != 0)
                def _(sz=sz, done=done):
                    act(lo + done, sz)
                done = done + (n & sz)

        def zero_copy(r0, sz):
            return pltpu.make_async_copy(zeros.at[pl.ds(0, sz * SUBL), :],
                                         xs_ref.at[pl.ds(pl.multiple_of(r0 * SUBL, SUBL), sz * SUBL), :], zsem)

        def start_all(e, carry):
            pieces(e, lambda r0, sz: zero_copy(r0, sz).start())
            return carry

        def wait_all(e, carry):
            pieces(e, lambda r0, sz: zero_copy(r0, sz).wait())
            return carry

        lax.fori_loop(0, N_EXPERTS, start_all, 0)
        lax.fori_loop(0, N_EXPERTS, wait_all, 0)

    te = te_ref[...]
    lane = lax.broadcasted_iota(I32, (tm, LANES), 1)
    hot = [lane == te[:, k:k + 1] for k in range(TOP_K)]
    oh = jnp.where(hot[0] | hot[1] | hot[2] | hot[3], 1.0, 0.0)
    row = lax.broadcasted_iota(I32, (tm, tm), 0)
    col = lax.broadcasted_iota(I32, (tm, tm), 1)
    before = jnp.dot((row > col).astype(BF16), oh.astype(BF16), preferred_element_type=F32)
    lane1 = lax.broadcasted_iota(I32, (1, LANES), 1)
    offs = jnp.zeros((1, LANES), F32)
    starts = []
    run = jnp.int32(0)
    for e in range(N_EXPERTS):
        starts.append(run)
        offs = jnp.where(lane1 == e, run.astype(F32), offs)
        run = run + cnt_ref[b * N_EXPERTS + e]
    pos = before + offs
    where_to = jnp.zeros((tm, LANES), F32)
    for k in range(TOP_K):
        j_k = jnp.sum(jnp.where(hot[k], pos, 0.0), axis=-1, keepdims=True)
        where_to = jnp.where(lane == k, j_k, where_to)
    slots = where_to.T.astype(I32)
    rid = lax.broadcasted_iota(I32, (pairs, tm), 0)
    pick = rid == slots[0:1, :]
    for k in range(1, TOP_K):
        pick = pick | (rid == slots[k:k + 1, :])
    perm = jnp.where(pick, 1.0, 0.0).astype(BF16)
    sorted_rows = jnp.dot(perm, x_ref[...].astype(BF16), preferred_element_type=F32)
    _to_tiles(xsort.at[slot], sorted_rows)

    sizes = [1 << s for s in range(tm.bit_length() - 1, -1, -1)]
    for e in range(N_EXPERTS):
        c = cnt_ref[b * N_EXPERTS + e]
        dst = base_ref[b * N_EXPERTS + e]
        done = jnp.int32(0)
        for sz in sizes:
            @pl.when((c & sz) != 0)
            def _(sz=sz, done=done, e=e, dst=dst):
                src0 = pl.multiple_of((starts[e] + done) * SUBL, SUBL)
                dst0 = pl.multiple_of((dst + done) * SUBL, SUBL)
                pltpu.make_async_copy(xsort.at[slot, pl.ds(src0, sz * SUBL), :],
                                      xs_ref.at[pl.ds(dst0, sz * SUBL), :], sem.at[slot]).start()
            done = done + (c & sz)
    def drain(which):
        pltpu.make_async_copy(xsort.at[which], xs_ref.at[pl.ds(0, pairs * SUBL), :], sem.at[which]).wait()

    @pl.when(b > 0)
    def _():
        drain(1 - slot)

    @pl.when(b == pl.num_programs(0) - 1)
    def _():
        drain(slot)


def _dispatch(x1, te, base, cnt, seg_lo, seg_hi, n_rows, tm):
    n = x1.shape[0]
    kern = functools.partial(_dispatch_kernel, tm=tm)
    grid_spec = pltpu.PrefetchScalarGridSpec(
        num_scalar_prefetch=4,
        grid=(n // tm,),
        in_specs=[pl.BlockSpec((tm, D_MODEL), lambda i, *_: (i, 0)),
                  pl.BlockSpec((tm, LANES), lambda i, *_: (i, 0))],
        out_specs=pl.BlockSpec(memory_space=pl.ANY),
        scratch_shapes=[pltpu.VMEM((2, tm * TOP_K * SUBL, LANES), F32), pltpu.VMEM((tm * SUBL, LANES), F32),
                        pltpu.SemaphoreType.DMA((2,)), pltpu.SemaphoreType.DMA(())],
    )
    return pl.pallas_call(
        kern,
        grid_spec=grid_spec,
        out_shape=jax.ShapeDtypeStruct((n_rows * SUBL, LANES), F32),
        compiler_params=_cparams(("arbitrary",)),
        name="moe_dispatch",
    )(base, cnt, seg_lo, seg_hi, x1, te)


FF_CHUNK = 512


def _expert_kernel(be_ref, nused_ref, nxt_ref, xs_ref, wgu_hbm, bgu_ref, wd_hbm, bd_ref, ys_ref,
                   wgu_b, wd_b, wgu_f, wd_f, wsem, *, tm, layer):
    i = pl.program_id(0)
    live = i < nused_ref[0]
    fresh = jnp.logical_or(i == 0, be_ref[i] != be_ref[jnp.maximum(i - 1, 0)])

    def weight_copies(e):
        w = layer * N_EXPERTS + e
        return (pltpu.make_async_copy(wgu_hbm.at[w], wgu_f, wsem.at[0]),
                pltpu.make_async_copy(wd_hbm.at[w], wd_f, wsem.at[1]))

    @pl.when(i == 0)
    def _():
        for cp in weight_copies(be_ref[0]):
            cp.start()

    @pl.when(jnp.logical_and(live, fresh))
    def _():
        for cp in weight_copies(be_ref[i]):
            cp.wait()
        wgu_b[...] = wgu_f[...].astype(BF16)
        wd_b[...] = wd_f[...].astype(BF16)

        @pl.when(nxt_ref[i] >= 0)
        def _():
            for cp in weight_copies(nxt_ref[i]):
                cp.start()

    @pl.when(live)
    def _():
        x16 = _from_tiles(xs_ref, 0, tm).astype(BF16)
        y = None
        for c in range(D_FF // FF_CHUNK):
            lo, hi = c * FF_CHUNK, (c + 1) * FF_CHUNK
            hg = jnp.dot(x16, wgu_b[:, lo:hi], preferred_element_type=F32) + bgu_ref[0, :, lo:hi]
            hu = (jnp.dot(x16, wgu_b[:, D_FF + lo:D_FF + hi], preferred_element_type=F32)
                  + bgu_ref[0, :, D_FF + lo:D_FF + hi])
            gate = jnp.minimum(hg, SWIGLU_LIMIT)
            up = jnp.clip(hu, -SWIGLU_LIMIT, SWIGLU_LIMIT)
            act = ((up + 1.0) * (gate * _sigmoid(SWIGLU_ALPHA * gate))).astype(BF16)
            part = jnp.dot(act, wd_b[lo:hi, :], preferred_element_type=F32)
            y = part if y is None else y + part
        _to_tiles(ys_ref, y + bd_ref[0])

    @pl.when(jnp.logical_not(live))
    def _():
        ys_ref[...] = jnp.zeros(ys_ref.shape, F32)


def _experts(xs, block_e, nused, next_e, wgu, bgu, wd, bd, layer, tm):
    n_rows = xs.shape[0] // SUBL
    nb = n_rows // tm

    def rows(i, be, nu, nx):
        return (jnp.minimum(i, nu[0] - 1), 0)

    def per_e(i, be, nu, nx):
        return (layer * N_EXPERTS + be[i], 0, 0)

    grid_spec = pltpu.PrefetchScalarGridSpec(
        num_scalar_prefetch=3,
        grid=(nb,),
        in_specs=[pl.BlockSpec((tm * SUBL, LANES), rows),
                  pl.BlockSpec(memory_space=pl.ANY),
                  pl.BlockSpec((1, 1, 2 * D_FF), per_e),
                  pl.BlockSpec(memory_space=pl.ANY),
                  pl.BlockSpec((1, 1, D_MODEL), per_e)],
        out_specs=pl.BlockSpec((tm * SUBL, LANES), lambda i, be, nu, nx: (i, 0)),
        scratch_shapes=[pltpu.VMEM((D_MODEL, 2 * D_FF), BF16), pltpu.VMEM((D_FF, D_MODEL), BF16),
                        pltpu.VMEM((D_MODEL, 2 * D_FF), F32), pltpu.VMEM((D_FF, D_MODEL), F32),
                        pltpu.SemaphoreType.DMA((2,))],
    )
    return pl.pallas_call(
        functools.partial(_expert_kernel, tm=tm, layer=layer),
        grid_spec=grid_spec,
        out_shape=jax.ShapeDtypeStruct((n_rows * SUBL, LANES), F32),
        compiler_params=_cparams(("arbitrary",)),
        name="moe_experts",
    )(block_e, nused, next_e, xs, wgu, bgu, wd, bd)


def _combine_kernel(dcur_ref, dnext_ref, ys_ref, x1_ref, tg_ref, g_ref, b_ref, o_ref, buf, sem, *, tm):
    i = pl.program_id(0)
    last = pl.num_programs(0) - 1
    p = i % 2
    rows = TOP_K * tm * SUBL

    def fetch(dest_ref, slot):
        def issue(t, carry):
            for k in range(TOP_K):
                pltpu.make_async_copy(_tile(ys_ref, dest_ref[t * TOP_K + k]),
                                      _tile(buf.at[slot], k * tm + t), sem.at[slot]).start(priority=k % 2)
            return carry

        lax.fori_loop(0, tm, issue, 0)

    @pl.when(i == 0)
    def _():
        fetch(dcur_ref, 0)

    @pl.when(i < last)
    def _():
        fetch(dnext_ref, 1 - p)

    pltpu.make_async_copy(ys_ref.at[pl.ds(0, rows), :], buf.at[p], sem.at[p]).wait()
    tg = tg_ref[...]
    cur = buf.at[p]
    y = tg[:, 0:1] * _from_tiles(cur, 0, tm)
    for k in range(1, TOP_K):
        y = y + tg[:, k:k + 1] * _from_tiles(cur, k * tm, tm)
    o_ref[...] = _layer_norm(DN_ALPHA * x1_ref[...] + y, g_ref[...], b_ref[...])


def _combine(ys, dest_flat, x1, tg, ln_g, ln_b, tm):
    n = x1.shape[0]
    nsteps = n // tm
    kern = functools.partial(_combine_kernel, tm=tm)
    return pl.pallas_call(
        kern,
        grid=(nsteps,),
        in_specs=[pl.BlockSpec((tm * TOP_K,), lambda i: (i,), memory_space=pltpu.SMEM),
                  pl.BlockSpec((tm * TOP_K,), lambda i: (jnp.minimum(i + 1, nsteps - 1),),
                               memory_space=pltpu.SMEM),
                  pl.BlockSpec(memory_space=pl.ANY),
                  pl.BlockSpec((tm, D_MODEL), lambda i: (i, 0)),
                  pl.BlockSpec((tm, LANES), lambda i: (i, 0)),
                  pl.BlockSpec((1, D_MODEL), lambda i: (0, 0)),
                  pl.BlockSpec((1, D_MODEL), lambda i: (0, 0))],
        out_specs=pl.BlockSpec((tm, D_MODEL), lambda i: (i, 0)),
        out_shape=jax.ShapeDtypeStruct((n, D_MODEL), F32),
        scratch_shapes=[pltpu.VMEM((2, TOP_K * tm * SUBL, LANES), F32), pltpu.SemaphoreType.DMA((2,))],
        compiler_params=_cparams(("arbitrary",)),
        name="moe_combine",
    )(dest_flat, dest_flat, ys, x1, tg, ln_g, ln_b)


def _pad_lanes(v, width=LANES, value=0.0):
    return jnp.pad(v, ((0, 0), (0, width - v.shape[-1])), constant_values=value)


def _prep_w_in(w):
    body = w[:, :6 * SEC]
    gi = _pad_lanes(w[:, 6 * SEC:6 * SEC + ML_HEADS])
    gf = _pad_lanes(w[:, 6 * SEC + ML_HEADS:6 * SEC + 2 * ML_HEADS])
    return jnp.concatenate([body, gi, gf], axis=1).astype(BF16)


def _layer(x, p, big, layer, lam_init, cfg):
    n = x.shape[0]
    b, s = cfg["batch"], cfg["seq"]
    tm_e = cfg["tm_expert"]

    qkv, ml, gates = _inproj(x, _prep_w_in(p["w_in"]), cfg["tm_proj"])

    lamv = jnp.zeros((8, LANES), F32)
    for r, nm in enumerate(("lam_q1", "lam_k1", "lam_q2", "lam_k2")):
        lamv = lamv.at[r, :DA_QK_DIM].set(p[nm])
    da = _diff_attention(qkv.reshape(b, s, 3 * SEC), lamv, p["da_norm_g"].reshape(1, LANES),
                         lam_init, cfg["tq"], cfg["tk"])

    gate_b = jnp.concatenate([_pad_lanes(p["gate_b"][None, :ML_HEADS]),
                              _pad_lanes(p["gate_b"][None, ML_HEADS:])], axis=1)
    mlo = _mlstm(ml.reshape(b, s, 3 * SEC), gates.reshape(b, s, GATE_W), p["conv_w"],
                 p["conv_b"].reshape(1, SEC), gate_b, cfg["chunk"], cfg["ml_group"])

    wr = _pad_lanes(p["w_router"])
    wr_hi = wr.astype(BF16)
    wr = jnp.concatenate([wr_hi, (wr - wr_hi.astype(F32)).astype(BF16)], axis=1)
    br = _pad_lanes(p["b_router"][None, :], value=NEG_INF)
    x1, te, tg, rk, cnt, bcnt = _outproj_router(
        da.reshape(n, SEC), mlo.reshape(n, SEC), x, p["w_out"].astype(BF16),
        p["ln1_g"].reshape(1, D_MODEL), p["ln1_b"].reshape(1, D_MODEL), wr, br, cfg["tm_route"])

    counts = cnt[0, :N_EXPERTS]
    padded = (counts + tm_e - 1) // tm_e * tm_e
    pad_ends = jnp.cumsum(padded)
    pad_starts = pad_ends - padded
    top_e = te[:, :TOP_K]
    sel = top_e[:, :, None] == jnp.arange(N_EXPERTS, dtype=I32)[None, None, :]
    dest = rk[:, :TOP_K] + jnp.sum(jnp.where(sel, pad_starts[None, None, :], 0), axis=-1)
    dest_flat = dest.reshape(n * TOP_K).astype(I32)
    n_rows = n * TOP_K + N_EXPERTS * tm_e
    nb = n_rows // tm_e
    blk_start = jnp.arange(nb, dtype=I32) * tm_e
    block_e = jnp.minimum(jnp.sum(blk_start[:, None] >= pad_ends[None, :], axis=-1),
                          N_EXPERTS - 1).astype(I32)
    nused = (pad_ends[-1:] // tm_e).astype(I32)
    ids = jnp.where(padded > 0, jnp.arange(N_EXPERTS, dtype=I32), N_EXPERTS)
    later = jnp.concatenate([lax.cummin(ids[::-1])[::-1][1:], jnp.full((1,), N_EXPERTS, I32)])
    next_e = jnp.where(later >= N_EXPERTS, -1, later)[block_e].astype(I32)
    seg_lo = (pad_starts + counts).astype(I32)
    seg_hi = pad_ends.at[N_EXPERTS - 1].set(n_rows).astype(I32)

    blk_cnt = bcnt[::8, :N_EXPERTS]
    blk_base = pad_starts[None, :] + jnp.cumsum(blk_cnt, axis=0) - blk_cnt
    xs = _dispatch(x1, te, blk_base.reshape(-1).astype(I32), blk_cnt.reshape(-1).astype(I32),
                   seg_lo, seg_hi, n_rows, cfg["tm_route"])
    ys = _experts(xs, block_e, nused, next_e, big["w_gu"], big["b_gu"], big["w_down"],
                  big["b_down"], layer, tm_e)
    return _combine(ys, dest_flat, x1, tg, p["ln2_g"].reshape(1, D_MODEL),
                    p["ln2_b"].reshape(1, D_MODEL), cfg["tm_comb"])


def _forward(x, params, cfg):
    b, s, d = x.shape
    h = x.reshape(b * s, d)
    depth = params["w_in"].shape[0]
    big = dict(w_gu=params["w_gu"].reshape(depth * N_EXPERTS, D_MODEL, 2 * D_FF),
               b_gu=params["b_gu"].reshape(depth * N_EXPERTS, 1, 2 * D_FF),
               w_down=params["w_down"].reshape(depth * N_EXPERTS, D_FF, D_MODEL),
               b_down=params["b_down"].reshape(depth * N_EXPERTS, 1, D_MODEL))
    for l in range(depth):
        lam_init = 0.8 - 0.6 * math.exp(-0.3 * l)
        p = {k: v[l] for k, v in params.items() if k not in big}
        h = _layer(h, p, big, l, lam_init, cfg)
    return h.reshape(b, s, d)


def kernel(x, w_in, conv_w, conv_b, gate_b, lam_q1, lam_k1, lam_q2, lam_k2, da_norm_g, w_out,
           ln1_g, ln1_b, w_router, b_router, w_gu, b_gu, w_down, b_down, ln2_g, ln2_b):
    params = dict(w_in=w_in, conv_w=conv_w, conv_b=conv_b, gate_b=gate_b, lam_q1=lam_q1,
                  lam_k1=lam_k1, lam_q2=lam_q2, lam_k2=lam_k2, da_norm_g=da_norm_g, w_out=w_out,
                  ln1_g=ln1_g, ln1_b=ln1_b, w_router=w_router, b_router=b_router, w_gu=w_gu,
                  b_gu=b_gu, w_down=w_down, b_down=b_down, ln2_g=ln2_g, ln2_b=ln2_b)
    cfg = dict(batch=x.shape[0], seq=x.shape[1], tm_proj=512, tm_route=512, tq=2048, tk=512, chunk=256, ml_group=4,
               tm_expert=512, tm_comb=256)
    return _forward(x, params, cfg)
```

```python
import functools
import math

import jax
import jax.numpy as jnp
from jax import lax
from jax.experimental import pallas as pl
from jax.experimental.pallas import tpu as pltpu

F32 = jnp.float32
BF16 = jnp.bfloat16
I32 = jnp.int32

D_MODEL = 1024
DEPTH = 4
DA_HEADS = 4
DA_QK_DIM = 64
DA_V_DIM = 128
ML_HEADS = 4
ML_QK_DIM = 64
ML_V_DIM = 128
CONV_WIDTH = 4
N_EXPERTS = 32
TOP_K = 4
D_FF = 1024
SWIGLU_LIMIT = 7.0
SWIGLU_ALPHA = 1.702
DN_ALPHA = (2 * DEPTH) ** 0.25
LN_EPS = 1e-5
RMS_EPS = 1e-5

LANES = 128
SEC = 512
GATE_W = 2 * LANES
IN_W_PAD = 6 * SEC + GATE_W
VMEM_LIMIT = 56 * 1024 * 1024
NEG_INF = float("-inf")
DA_Q_SCALE = DA_QK_DIM ** -0.5 * math.log2(math.e)


def _cparams(sem):
    return pltpu.CompilerParams(dimension_semantics=sem, vmem_limit_bytes=VMEM_LIMIT)


def _inproj_kernel(x_ref, w_ref, qkv_ref, ml_ref, gate_ref):
    xb = x_ref[...].astype(BF16)
    for c in range(3):
        sl = slice(c * SEC, (c + 1) * SEC)
        r = jnp.dot(xb, w_ref[:, sl], preferred_element_type=F32)
        if c == 0:
            r = r * DA_Q_SCALE
        qkv_ref[:, sl] = r.astype(BF16)
    for c in range(3):
        ml_ref[:, c * SEC:(c + 1) * SEC] = jnp.dot(
            xb, w_ref[:, (3 + c) * SEC:(4 + c) * SEC], preferred_element_type=F32)
    gate_ref[...] = jnp.dot(xb, w_ref[:, 6 * SEC:], preferred_element_type=F32)


def _inproj(x, w, tm):
    n = x.shape[0]
    return pl.pallas_call(
        _inproj_kernel,
        grid=(n // tm,),
        in_specs=[pl.BlockSpec((tm, D_MODEL), lambda i: (i, 0)),
                  pl.BlockSpec((D_MODEL, IN_W_PAD), lambda i: (0, 0))],
        out_specs=[pl.BlockSpec((tm, 3 * SEC), lambda i: (i, 0)),
                   pl.BlockSpec((tm, 3 * SEC), lambda i: (i, 0)),
                   pl.BlockSpec((tm, GATE_W), lambda i: (i, 0))],
        out_shape=[jax.ShapeDtypeStruct((n, 3 * SEC), BF16),
                   jax.ShapeDtypeStruct((n, 3 * SEC), F32),
                   jax.ShapeDtypeStruct((n, GATE_W), F32)],
        compiler_params=_cparams(("parallel",)),
        name="in_proj",
    )(x, w)


def _attn_kernel(qi_tab, ki_tab, q_ref, k_ref, v_ref, lamv_ref, g_ref, o_ref, qs, m_s, acc,
                 *, lam_init, tq, tk):
    step = pl.program_id(2)
    qi = qi_tab[step]
    ki = ki_tab[step]
    ng = tq // tk
    reps = tk // LANES

    @pl.when(ki == 0)
    def _():
        q = q_ref[0]
        lane = lax.broadcasted_iota(I32, q.shape, 1)
        zero = jnp.zeros_like(q)
        qs[0:tq, :] = jnp.where(lane < DA_QK_DIM, q, zero)
        qs[tq:2 * tq, :] = jnp.where(lane >= DA_QK_DIM, q, zero)
        m_s[...] = jnp.full(m_s.shape, NEG_INF, F32)
        acc[...] = jnp.zeros(acc.shape, F32)

    def update(diag):
        k = k_ref[0]
        v_aug = jnp.concatenate([v_ref[0], jnp.ones((tk, LANES), BF16)], axis=1)
        groups = [(gi, slice(mp * tq + gi * tk, mp * tq + (gi + 1) * tk))
                  for gi in range(ng) if gi >= diag for mp in range(2)]
        ss = [lax.dot_general(qs[g, :], k, (((1,), (1,)), ((), ())), preferred_element_type=F32)
              for _, g in groups]
        if diag >= 0:
            row = lax.broadcasted_iota(I32, (tk, tk), 0)
            col = lax.broadcasted_iota(I32, (tk, tk), 1)
            ss = [jnp.where(row >= col, s, NEG_INF) if gi == diag else s
                  for (gi, _), s in zip(groups, ss)]
        for (_, g), s in zip(groups, ss):
            m_old = m_s[g, :]
            m_new = jnp.maximum(m_old, jnp.max(s, axis=-1, keepdims=True))
            p = jnp.exp2(s - jnp.tile(m_new, (1, reps)))
            alpha = jnp.exp2(m_old - m_new)
            pv = jnp.dot(p.astype(BF16), v_aug, preferred_element_type=F32)
            acc[g, :] = jnp.tile(alpha, (1, 2)) * acc[g, :] + pv
            m_s[g, :] = m_new

    @pl.when(ki < qi * ng)
    def _():
        update(-1)

    for d in range(ng):
        @pl.when(ki == qi * ng + d)
        def _(d=d):
            update(d)

    @pl.when(ki == qi * ng + ng - 1)
    def _():
        lamv = lamv_ref[...]
        lam = (jnp.exp(jnp.sum(lamv[0:1] * lamv[1:2], axis=-1, keepdims=True))
               - jnp.exp(jnp.sum(lamv[2:3] * lamv[3:4], axis=-1, keepdims=True)) + lam_init)
        on = acc[:, 0:LANES] / acc[:, LANES:2 * LANES]
        o = on[0:tq] - lam * on[tq:2 * tq]
        ms = jnp.mean(o * o, axis=-1, keepdims=True)
        o = o * lax.rsqrt(ms + RMS_EPS) * g_ref[...] * (1.0 - lam_init)
        o_ref[0] = o.astype(o_ref.dtype)


def _diff_attention(qkv, lamv, norm_g, lam_init, tq, tk):
    b, s, _ = qkv.shape
    ng = tq // tk
    pairs = [(qi, ki) for qi in range(s // tq) for ki in range((qi + 1) * ng)]
    qi_tab = jnp.asarray([p[0] for p in pairs], I32)
    ki_tab = jnp.asarray([p[1] for p in pairs], I32)
    kern = functools.partial(_attn_kernel, lam_init=lam_init, tq=tq, tk=tk)
    grid_spec = pltpu.PrefetchScalarGridSpec(
        num_scalar_prefetch=2,
        grid=(b, DA_HEADS, len(pairs)),
        in_specs=[
            pl.BlockSpec((1, tq, LANES), lambda bi, h, t, qt, kt: (bi, qt[t], h)),
            pl.BlockSpec((1, tk, LANES), lambda bi, h, t, qt, kt: (bi, kt[t], DA_HEADS + h)),
            pl.BlockSpec((1, tk, LANES), lambda bi, h, t, qt, kt: (bi, kt[t], 2 * DA_HEADS + h)),
            pl.BlockSpec((8, LANES), lambda bi, h, t, qt, kt: (0, 0)),
            pl.BlockSpec((1, LANES), lambda bi, h, t, qt, kt: (0, 0)),
        ],
        out_specs=pl.BlockSpec((1, tq, LANES), lambda bi, h, t, qt, kt: (bi, qt[t], h)),
        scratch_shapes=[pltpu.VMEM((2 * tq, LANES), BF16), pltpu.VMEM((2 * tq, LANES), F32),
                        pltpu.VMEM((2 * tq, 2 * LANES), F32)],
    )
    return pl.pallas_call(
        kern,
        grid_spec=grid_spec,
        out_shape=jax.ShapeDtypeStruct((b, s, SEC), BF16),
        compiler_params=_cparams(("parallel", "parallel", "arbitrary")),
        name="diff_attn",
    )(qi_tab, ki_tab, qkv, qkv, qkv, lamv, norm_g)


def _log_sigmoid(x):
    return jnp.minimum(x, 0.0) - jnp.log1p(jnp.exp(-jnp.abs(x)))


def _sigmoid(x):
    return 1.0 / (1.0 + jnp.exp(-x))


def _mlstm_kernel(ml_ref, gt_ref, cw_ref, cb_ref, gb_ref, o_ref, xbuf, cst, mst, *, chunk, group):
    @pl.when(pl.program_id(1) == 0)
    def _():
        xbuf[:, 0:ML_HALO, :] = jnp.zeros((group, ML_HALO, SEC), F32)
        cst[...] = jnp.zeros(cst.shape, F32)
        mst[...] = jnp.full(mst.shape, NEG_INF, F32)

    for bb in range(group):
        _mlstm_chunk(bb, ml_ref, gt_ref, cw_ref, cb_ref, gb_ref, o_ref, xbuf, cst, mst, chunk)


ML_HALO = 8


def _mlstm_chunk(bb, ml_ref, gt_ref, cw_ref, cb_ref, gb_ref, o_ref, xbuf, cst, mst, chunk):
    L = chunk
    HALO = ML_HALO

    xbuf[bb, HALO:HALO + L, :] = ml_ref[bb, :, 0:SEC]
    y = cb_ref[...] + cw_ref[CONV_WIDTH - 1:CONV_WIDTH, :] * xbuf[bb, HALO:HALO + L, :]
    for j in range(CONV_WIDTH - 1):
        off = HALO - (CONV_WIDTH - 1) + j
        y = y + cw_ref[j:j + 1, :] * xbuf[bb, off:off + L, :]
    xbuf[bb, 0:HALO, :] = xbuf[bb, L:L + HALO, :]
    qk = y * _sigmoid(y)

    gi = gt_ref[bb, :, 0:LANES] + gb_ref[:, 0:LANES]
    lf = _log_sigmoid(gt_ref[bb, :, LANES:GATE_W] + gb_ref[:, LANES:GATE_W])
    row = lax.broadcasted_iota(I32, (L, L), 0)
    col = lax.broadcasted_iota(I32, (L, L), 1)
    causal = row >= col
    bcum = jnp.dot(causal.astype(F32), lf, preferred_element_type=F32,
                   precision=lax.Precision.HIGHEST)
    g = bcum[L - 1:L, :]
    w_end = g - bcum + gi
    m_loc = jnp.max(w_end, axis=0, keepdims=True)
    e_end = jnp.exp(w_end - m_loc)
    m_prev = mst[bb]
    m_new = jnp.maximum(g + m_prev, m_loc)
    a_dec = jnp.exp(g + m_prev - m_new)
    b_dec = jnp.exp(m_loc - m_new)
    mst[bb] = m_new
    inter_log = bcum + m_prev
    bcum_t = bcum.T
    gi_t = gi.T

    lane = lax.broadcasted_iota(I32, (L, LANES), 1)
    sub = lax.broadcasted_iota(I32, (LANES, 1), 0)
    ones_col = jnp.where(lane == 0, 1.0, 0.0).astype(BF16)
    for p in range(ML_HEADS // 2):
        q_pair = qk[:, p * LANES:(p + 1) * LANES] * (ML_QK_DIM ** -0.5)
        k_pair = qk[:, 2 * LANES + p * LANES:2 * LANES + (p + 1) * LANES]
        kb = k_pair.astype(BF16)
        cp = bb * (ML_HEADS // 2) + p
        c_prev = cst[cp].astype(BF16)
        upd = jnp.zeros((LANES, 2 * LANES), F32)
        for hh in range(2):
            h = 2 * p + hh
            head_lanes = (lane >= hh * ML_QK_DIM) & (lane < (hh + 1) * ML_QK_DIM)
            qm = jnp.where(head_lanes, q_pair, 0.0).astype(BF16)
            s = lax.dot_general(qm, kb, (((1,), (1,)), ((), ())), preferred_element_type=F32)
            d = jnp.where(causal, bcum[:, h:h + 1] - bcum_t[h:h + 1, :] + gi_t[h:h + 1, :], NEG_INF)
            il = inter_log[:, h:h + 1]
            m_t = jnp.maximum(il, jnp.max(d, axis=-1, keepdims=True))
            sd = (s * jnp.exp(d - m_t)).astype(BF16)
            inter_w = jnp.exp(il - m_t)
            v_aug = jnp.concatenate(
                [ml_ref[bb, :, SEC + h * LANES:SEC + (h + 1) * LANES].astype(BF16), ones_col], axis=1)
            intra = jnp.dot(sd, v_aug, preferred_element_type=F32)
            inter = jnp.dot(qm, c_prev, preferred_element_type=F32)
            num = inter_w * inter[:, 0:LANES] + intra[:, 0:LANES]
            den = inter_w * inter[:, LANES:LANES + 1] + intra[:, LANES:LANES + 1]
            hid = num / jnp.maximum(jnp.abs(den), jnp.exp(-m_t))
            o_gate = ml_ref[bb, :, 2 * SEC + h * LANES:2 * SEC + (h + 1) * LANES]
            o_ref[bb, :, h * LANES:(h + 1) * LANES] = (hid * _sigmoid(o_gate)).astype(o_ref.dtype)
            ek = jnp.where(head_lanes, e_end[:, h:h + 1] * k_pair, 0.0).astype(BF16)
            upd = upd + lax.dot_general(ek, v_aug, (((0,), (0,)), ((), ())),
                                        preferred_element_type=F32)
        first = sub < ML_QK_DIM
        a_rows = jnp.where(first, a_dec[:, 2 * p:2 * p + 1], a_dec[:, 2 * p + 1:2 * p + 2])
        b_rows = jnp.where(first, b_dec[:, 2 * p:2 * p + 1], b_dec[:, 2 * p + 1:2 * p + 2])
        cst[cp] = a_rows * cst[cp] + b_rows * upd


def _mlstm(ml, gates, conv_w, conv_b, gate_b, chunk, group):
    b, s, _ = ml.shape
    kern = functools.partial(_mlstm_kernel, chunk=chunk, group=group)
    return pl.pallas_call(
        kern,
        grid=(b // group, s // chunk),
        in_specs=[pl.BlockSpec((group, chunk, 3 * SEC), lambda bi, c: (bi, c, 0)),
                  pl.BlockSpec((group, chunk, GATE_W), lambda bi, c: (bi, c, 0)),
                  pl.BlockSpec((CONV_WIDTH, SEC), lambda bi, c: (0, 0)),
                  pl.BlockSpec((1, SEC), lambda bi, c: (0, 0)),
                  pl.BlockSpec((1, GATE_W), lambda bi, c: (0, 0))],
        out_specs=pl.BlockSpec((group, chunk, SEC), lambda bi, c: (bi, c, 0)),
        out_shape=jax.ShapeDtypeStruct((b, s, SEC), BF16),
        scratch_shapes=[pltpu.VMEM((group, chunk + 8, SEC), F32),
                        pltpu.VMEM((group * (ML_HEADS // 2), LANES, 2 * LANES), F32),
                        pltpu.VMEM((group, 1, LANES), F32)],
        compiler_params=_cparams(("parallel", "arbitrary")),
        name="mlstm",
    )(ml, gates, conv_w, conv_b, gate_b)


SUBL = D_MODEL // LANES


def _to_tiles(ref, val, base=0):
    rows = val.shape[0]
    for s in range(SUBL):
        ref[pl.ds(base * SUBL + s, rows, stride=SUBL), :] = val[:, s * LANES:(s + 1) * LANES]


def _tile(ref, t):
    return ref.at[pl.ds(pl.multiple_of(t * SUBL, SUBL), SUBL), :]


def _from_tiles(ref, base, rows):
    return jnp.concatenate(
        [ref[pl.ds(base * SUBL + s, rows, stride=SUBL), :] for s in range(SUBL)], axis=1)


def _layer_norm(h, g, b):
    mu = jnp.mean(h, axis=-1, keepdims=True)
    hc = h - mu
    var = jnp.mean(hc * hc, axis=-1, keepdims=True)
    return hc * lax.rsqrt(var + LN_EPS) * g + b


def _outproj_router_kernel(da_ref, mlo_ref, x_ref, wo_ref, g_ref, b_ref, wr_ref, br_ref,
                           x1_ref, te_ref, tg_ref, rk_ref, cnt_ref, bc_ref, carry, *, tm):
    i = pl.program_id(0)

    @pl.when(i == 0)
    def _():
        carry[...] = jnp.zeros(carry.shape, F32)

    mix = (jnp.dot(da_ref[...], wo_ref[0:SEC, :], preferred_element_type=F32)
           + jnp.dot(mlo_ref[...], wo_ref[SEC:2 * SEC, :], preferred_element_type=F32))
    x1 = _layer_norm(DN_ALPHA * x_ref[...] + mix, g_ref[...], b_ref[...])
    x1_ref[...] = x1

    x_hi = x1.astype(BF16)
    x_lo = (x1 - x_hi.astype(F32)).astype(BF16)
    hh_hl = jnp.dot(x_hi, wr_ref[...], preferred_element_type=F32)
    lh = jnp.dot(x_lo, wr_ref[:, 0:LANES], preferred_element_type=F32)
    logits = hh_hl[:, 0:LANES] + (hh_hl[:, LANES:2 * LANES] + lh) + br_ref[...]
    lane = lax.broadcasted_iota(I32, (tm, LANES), 1)
    lane_f = lane.astype(F32)
    work = logits
    hot = []
    vals = []
    idxs = []
    for _k in range(TOP_K):
        mx = jnp.max(work, axis=-1, keepdims=True)
        idx = jnp.min(jnp.where(work == mx, lane_f, float(LANES)), axis=-1, keepdims=True)
        sel = lane_f == idx
        hot.append(sel)
        vals.append(mx)
        idxs.append(idx)
        work = jnp.where(sel, NEG_INF, work)
    exps = [jnp.exp(v - vals[0]) for v in vals]
    inv = 1.0 / (exps[0] + exps[1] + exps[2] + exps[3])

    onehot = (hot[0] | hot[1] | hot[2] | hot[3])
    oh = jnp.where(onehot, 1.0, 0.0)
    row = lax.broadcasted_iota(I32, (tm, tm), 0)
    col = lax.broadcasted_iota(I32, (tm, tm), 1)
    before = jnp.dot((row > col).astype(BF16), oh.astype(BF16), preferred_element_type=F32)
    pos = before + carry[...]
    te = jnp.zeros((tm, LANES), I32)
    tg = jnp.zeros((tm, LANES), F32)
    rk = jnp.zeros((tm, LANES), I32)
    for k in range(TOP_K):
        slot = lane == k
        r_k = jnp.sum(jnp.where(hot[k], pos, 0.0), axis=-1, keepdims=True)
        te = jnp.where(slot, idxs[k].astype(I32), te)
        tg = jnp.where(slot, exps[k] * inv, tg)
        rk = jnp.where(slot, r_k.astype(I32), rk)
    te_ref[...] = te
    tg_ref[...] = tg
    rk_ref[...] = rk
    in_block = jnp.sum(oh, axis=0, keepdims=True)
    bc_ref[...] = jnp.broadcast_to(in_block, bc_ref.shape).astype(I32)
    carry[...] = carry[...] + in_block
    cnt_ref[...] = jnp.broadcast_to(carry[...], cnt_ref.shape).astype(I32)


def _outproj_router(da, mlo, x, wo, ln_g, ln_b, wr, br, tm):
    n = x.shape[0]
    kern = functools.partial(_outproj_router_kernel, tm=tm)
    row_blk = lambda w: pl.BlockSpec((tm, w), lambda i: (i, 0))
    const = lambda r, w: pl.BlockSpec((r, w), lambda i: (0, 0))
    return pl.pallas_call(
        kern,
        grid=(n // tm,),
        in_specs=[row_blk(SEC), row_blk(SEC), row_blk(D_MODEL), const(2 * SEC, D_MODEL),
                  const(1, D_MODEL), const(1, D_MODEL), const(D_MODEL, 2 * LANES), const(1, LANES)],
        out_specs=[row_blk(D_MODEL), row_blk(LANES), row_blk(LANES), row_blk(LANES), const(8, LANES),
                   pl.BlockSpec((8, LANES), lambda i: (i, 0))],
        out_shape=[jax.ShapeDtypeStruct((n, D_MODEL), F32),
                   jax.ShapeDtypeStruct((n, LANES), I32),
                   jax.ShapeDtypeStruct((n, LANES), F32),
                   jax.ShapeDtypeStruct((n, LANES), I32),
                   jax.ShapeDtypeStruct((8, LANES), I32),
                   jax.ShapeDtypeStruct((n // tm * 8, LANES), I32)],
        scratch_shapes=[pltpu.VMEM((1, LANES), F32)],
        compiler_params=_cparams(("arbitrary",)),
        name="outproj_router",
    )(da, mlo, x, wo, ln_g, ln_b, wr, br)


def _dispatch_kernel(base_ref, cnt_ref, lo_ref, hi_ref, x_ref, te_ref, xs_ref, xsort, zeros, sem, zsem,
                     *, tm):
    b = pl.program_id(0)
    pairs = tm * TOP_K
    slot = b % 2

    @pl.when(b == 0)
    def _():
        zeros[...] = jnp.zeros(zeros.shape, F32)
        tm_shift = tm.bit_length() - 1

        def pieces(e, act):
            lo = lo_ref[e]
            n = hi_ref[e] - lo
            whole = lax.shift_right_logical(n, tm_shift)

            def full(j, c):
                act(lo + j * tm, tm)
                return c

            lax.fori_loop(0, whole, full, 0)
            done = whole * tm
            for sz in [1 << s for s in range(tm_shift - 1, -1, -1)]:
                @pl.when((n & sz) != 0)
                def _(sz=sz, done=done):
                    act(lo + done, sz)
                done = done + (n & sz)

        def zero_copy(r0, sz):
            return pltpu.make_async_copy(zeros.at[pl.ds(0, sz * SUBL), :],
                                         xs_ref.at[pl.ds(pl.multiple_of(r0 * SUBL, SUBL), sz * SUBL), :], zsem)

        def start_all(e, carry):
            pieces(e, lambda r0, sz: zero_copy(r0, sz).start())
            return carry

        def wait_all(e, carry):
            pieces(e, lambda r0, sz: zero_copy(r0, sz).wait())
            return carry

        lax.fori_loop(0, N_EXPERTS, start_all, 0)
        lax.fori_loop(0, N_EXPERTS, wait_all, 0)

    te = te_ref[...]
    lane = lax.broadcasted_iota(I32, (tm, LANES), 1)
    hot = [lane == te[:, k:k + 1] for k in range(TOP_K)]
    oh = jnp.where(hot[0] | hot[1] | hot[2] | hot[3], 1.0, 0.0)
    row = lax.broadcasted_iota(I32, (tm, tm), 0)
    col = lax.broadcasted_iota(I32, (tm, tm), 1)
    before = jnp.dot((row > col).astype(BF16), oh.astype(BF16), preferred_element_type=F32)
    lane1 = lax.broadcasted_iota(I32, (1, LANES), 1)
    offs = jnp.zeros((1, LANES), F32)
    starts = []
    run = jnp.int32(0)
    for e in range(N_EXPERTS):
        starts.append(run)
        offs = jnp.where(lane1 == e, run.astype(F32), offs)
        run = run + cnt_ref[b * N_EXPERTS + e]
    pos = before + offs
    where_to = jnp.zeros((tm, LANES), F32)
    for k in range(TOP_K):
        j_k = jnp.sum(jnp.where(hot[k], pos, 0.0), axis=-1, keepdims=True)
        where_to = jnp.where(lane == k, j_k, where_to)
    slots = where_to.T.astype(I32)
    rid = lax.broadcasted_iota(I32, (pairs, tm), 0)
    perm = jnp.zeros((pairs, tm), F32)
    for k in range(TOP_K):
        perm = jnp.where(rid == slots[k:k + 1, :], 1.0, perm)
    perm = perm.astype(BF16)
    sorted_rows = jnp.dot(perm, x_ref[...].astype(BF16), preferred_element_type=F32)
    _to_tiles(xsort.at[slot], sorted_rows)

    sizes = [1 << s for s in range(tm.bit_length() - 1, -1, -1)]
    for e in range(N_EXPERTS):
        c = cnt_ref[b * N_EXPERTS + e]
        dst = base_ref[b * N_EXPERTS + e]
        done = jnp.int32(0)
        for sz in sizes:
            @pl.when((c & sz) != 0)
            def _(sz=sz, done=done, e=e, dst=dst):
                src0 = pl.multiple_of((starts[e] + done) * SUBL, SUBL)
                dst0 = pl.multiple_of((dst + done) * SUBL, SUBL)
                pltpu.make_async_copy(xsort.at[slot, pl.ds(src0, sz * SUBL), :],
                                      xs_ref.at[pl.ds(dst0, sz * SUBL), :], sem.at[slot]).start()
            done = done + (c & sz)
    def drain(which):
        pltpu.make_async_copy(xsort.at[which], xs_ref.at[pl.ds(0, pairs * SUBL), :], sem.at[which]).wait()

    @pl.when(b > 0)
    def _():
        drain(1 - slot)

    @pl.when(b == pl.num_programs(0) - 1)
    def _():
        drain(slot)


def _dispatch(x1, te, base, cnt, seg_lo, seg_hi, n_rows, tm):
    n = x1.shape[0]
    kern = functools.partial(_dispatch_kernel, tm=tm)
    grid_spec = pltpu.PrefetchScalarGridSpec(
        num_scalar_prefetch=4,
        grid=(n // tm,),
        in_specs=[pl.BlockSpec((tm, D_MODEL), lambda i, *_: (i, 0)),
                  pl.BlockSpec((tm, LANES), lambda i, *_: (i, 0))],
        out_specs=pl.BlockSpec(memory_space=pl.ANY),
        scratch_shapes=[pltpu.VMEM((2, tm * TOP_K * SUBL, LANES), F32), pltpu.VMEM((tm * SUBL, LANES), F32),
                        pltpu.SemaphoreType.DMA((2,)), pltpu.SemaphoreType.DMA(())],
    )
    return pl.pallas_call(
        kern,
        grid_spec=grid_spec,
        out_shape=jax.ShapeDtypeStruct((n_rows * SUBL, LANES), F32),
        compiler_params=_cparams(("arbitrary",)),
        name="moe_dispatch",
    )(base, cnt, seg_lo, seg_hi, x1, te)


FF_CHUNK = 512


def _expert_kernel(be_ref, nused_ref, nxt_ref, xs_ref, wgu_hbm, bgu_ref, wd_hbm, bd_ref, ys_ref,
                   wgu_b, wd_b, wgu_f, wd_f, wsem, *, tm, layer):
    i = pl.program_id(0)
    live = i < nused_ref[0]
    fresh = jnp.logical_or(i == 0, be_ref[i] != be_ref[jnp.maximum(i - 1, 0)])

    def weight_copies(e):
        w = layer * N_EXPERTS + e
        return (pltpu.make_async_copy(wgu_hbm.at[w], wgu_f, wsem.at[0]),
                pltpu.make_async_copy(wd_hbm.at[w], wd_f, wsem.at[1]))

    @pl.when(i == 0)
    def _():
        for cp in weight_copies(be_ref[0]):
            cp.start()

    @pl.when(jnp.logical_and(live, fresh))
    def _():
        for cp in weight_copies(be_ref[i]):
            cp.wait()
        wgu_b[...] = wgu_f[...].astype(BF16)
        wd_b[...] = wd_f[...].astype(BF16)

        @pl.when(nxt_ref[i] >= 0)
        def _():
            for cp in weight_copies(nxt_ref[i]):
                cp.start()

    @pl.when(live)
    def _():
        x16 = _from_tiles(xs_ref, 0, tm).astype(BF16)
        y = None
        for c in range(D_FF // FF_CHUNK):
            lo, hi = c * FF_CHUNK, (c + 1) * FF_CHUNK
            hg = jnp.dot(x16, wgu_b[:, lo:hi], preferred_element_type=F32) + bgu_ref[0, :, lo:hi]
            hu = (jnp.dot(x16, wgu_b[:, D_FF + lo:D_FF + hi], preferred_element_type=F32)
                  + bgu_ref[0, :, D_FF + lo:D_FF + hi])
            gate = jnp.minimum(hg, SWIGLU_LIMIT)
            up = jnp.clip(hu, -SWIGLU_LIMIT, SWIGLU_LIMIT)
            act = ((up + 1.0) * (gate * _sigmoid(SWIGLU_ALPHA * gate))).astype(BF16)
            part = jnp.dot(act, wd_b[lo:hi, :], preferred_element_type=F32)
            y = part if y is None else y + part
        _to_tiles(ys_ref, y + bd_ref[0])

    @pl.when(jnp.logical_not(live))
    def _():
        ys_ref[...] = jnp.zeros(ys_ref.shape, F32)


def _experts(xs, block_e, nused, next_e, wgu, bgu, wd, bd, layer, tm):
    n_rows = xs.shape[0] // SUBL
    nb = n_rows // tm

    def rows(i, be, nu, nx):
        return (jnp.minimum(i, nu[0] - 1), 0)

    def per_e(i, be, nu, nx):
        return (layer * N_EXPERTS + be[i], 0, 0)

    grid_spec = pltpu.PrefetchScalarGridSpec(
        num_scalar_prefetch=3,
        grid=(nb,),
        in_specs=[pl.BlockSpec((tm * SUBL, LANES), rows),
                  pl.BlockSpec(memory_space=pl.ANY),
                  pl.BlockSpec((1, 1, 2 * D_FF), per_e),
                  pl.BlockSpec(memory_space=pl.ANY),
                  pl.BlockSpec((1, 1, D_MODEL), per_e)],
        out_specs=pl.BlockSpec((tm * SUBL, LANES), lambda i, be, nu, nx: (i, 0)),
        scratch_shapes=[pltpu.VMEM((D_MODEL, 2 * D_FF), BF16), pltpu.VMEM((D_FF, D_MODEL), BF16),
                        pltpu.VMEM((D_MODEL, 2 * D_FF), F32), pltpu.VMEM((D_FF, D_MODEL), F32),
                        pltpu.SemaphoreType.DMA((2,))],
    )
    return pl.pallas_call(
        functools.partial(_expert_kernel, tm=tm, layer=layer),
        grid_spec=grid_spec,
        out_shape=jax.ShapeDtypeStruct((n_rows * SUBL, LANES), F32),
        compiler_params=_cparams(("arbitrary",)),
        name="moe_experts",
    )(block_e, nused, next_e, xs, wgu, bgu, wd, bd)


def _combine_kernel(dcur_ref, dnext_ref, ys_ref, x1_ref, tg_ref, g_ref, b_ref, o_ref, buf, sem, *, tm):
    i = pl.program_id(0)
    last = pl.num_programs(0) - 1
    p = i % 2
    rows = TOP_K * tm * SUBL

    def fetch(dest_ref, slot):
        def issue(t, carry):
            for k in range(TOP_K):
                pltpu.make_async_copy(_tile(ys_ref, dest_ref[t * TOP_K + k]),
                                      _tile(buf.at[slot], k * tm + t), sem.at[slot]).start(priority=k % 2)
            return carry

        lax.fori_loop(0, tm, issue, 0)

    @pl.when(i == 0)
    def _():
        fetch(dcur_ref, 0)

    @pl.when(i < last)
    def _():
        fetch(dnext_ref, 1 - p)

    pltpu.make_async_copy(ys_ref.at[pl.ds(0, rows), :], buf.at[p], sem.at[p]).wait()
    tg = tg_ref[...]
    cur = buf.at[p]
    y = tg[:, 0:1] * _from_tiles(cur, 0, tm)
    for k in range(1, TOP_K):
        y = y + tg[:, k:k + 1] * _from_tiles(cur, k * tm, tm)
    o_ref[...] = _layer_norm(DN_ALPHA * x1_ref[...] + y, g_ref[...], b_ref[...])


def _combine(ys, dest_flat, x1, tg, ln_g, ln_b, tm):
    n = x1.shape[0]
    nsteps = n // tm
    kern = functools.partial(_combine_kernel, tm=tm)
    return pl.pallas_call(
        kern,
        grid=(nsteps,),
        in_specs=[pl.BlockSpec((tm * TOP_K,), lambda i: (i,), memory_space=pltpu.SMEM),
                  pl.BlockSpec((tm * TOP_K,), lambda i: (jnp.minimum(i + 1, nsteps - 1),),
                               memory_space=pltpu.SMEM),
                  pl.BlockSpec(memory_space=pl.ANY),
                  pl.BlockSpec((tm, D_MODEL), lambda i: (i, 0)),
                  pl.BlockSpec((tm, LANES), lambda i: (i, 0)),
                  pl.BlockSpec((1, D_MODEL), lambda i: (0, 0)),
                  pl.BlockSpec((1, D_MODEL), lambda i: (0, 0))],
        out_specs=pl.BlockSpec((tm, D_MODEL), lambda i: (i, 0)),
        out_shape=jax.ShapeDtypeStruct((n, D_MODEL), F32),
        scratch_shapes=[pltpu.VMEM((2, TOP_K * tm * SUBL, LANES), F32), pltpu.SemaphoreType.DMA((2,))],
        compiler_params=_cparams(("arbitrary",)),
        name="moe_combine",
    )(dest_flat, dest_flat, ys, x1, tg, ln_g, ln_b)


def _pad_lanes(v, width=LANES, value=0.0):
    return jnp.pad(v, ((0, 0), (0, width - v.shape[-1])), constant_values=value)


def _prep_w_in(w):
    body = w[:, :6 * SEC]
    gi = _pad_lanes(w[:, 6 * SEC:6 * SEC + ML_HEADS])
    gf = _pad_lanes(w[:, 6 * SEC + ML_HEADS:6 * SEC + 2 * ML_HEADS])
    return jnp.concatenate([body, gi, gf], axis=1).astype(BF16)


def _layer(x, p, big, layer, lam_init, cfg):
    n = x.shape[0]
    b, s = cfg["batch"], cfg["seq"]
    tm_e = cfg["tm_expert"]

    qkv, ml, gates = _inproj(x, _prep_w_in(p["w_in"]), cfg["tm_proj"])

    lamv = jnp.zeros((8, LANES), F32)
    for r, nm in enumerate(("lam_q1", "lam_k1", "lam_q2", "lam_k2")):
        lamv = lamv.at[r, :DA_QK_DIM].set(p[nm])
    da = _diff_attention(qkv.reshape(b, s, 3 * SEC), lamv, p["da_norm_g"].reshape(1, LANES),
                         lam_init, cfg["tq"], cfg["tk"])

    gate_b = jnp.concatenate([_pad_lanes(p["gate_b"][None, :ML_HEADS]),
                              _pad_lanes(p["gate_b"][None, ML_HEADS:])], axis=1)
    mlo = _mlstm(ml.reshape(b, s, 3 * SEC), gates.reshape(b, s, GATE_W), p["conv_w"],
                 p["conv_b"].reshape(1, SEC), gate_b, cfg["chunk"], cfg["ml_group"])

    wr = _pad_lanes(p["w_router"])
    wr_hi = wr.astype(BF16)
    wr = jnp.concatenate([wr_hi, (wr - wr_hi.astype(F32)).astype(BF16)], axis=1)
    br = _pad_lanes(p["b_router"][None, :], value=NEG_INF)
    x1, te, tg, rk, cnt, bcnt = _outproj_router(
        da.reshape(n, SEC), mlo.reshape(n, SEC), x, p["w_out"].astype(BF16),
        p["ln1_g"].reshape(1, D_MODEL), p["ln1_b"].reshape(1, D_MODEL), wr, br, cfg["tm_route"])

    counts = cnt[0, :N_EXPERTS]
    padded = (counts + tm_e - 1) // tm_e * tm_e
    pad_ends = jnp.cumsum(padded)
    pad_starts = pad_ends - padded
    top_e = te[:, :TOP_K]
    sel = top_e[:, :, None] == jnp.arange(N_EXPERTS, dtype=I32)[None, None, :]
    dest = rk[:, :TOP_K] + jnp.sum(jnp.where(sel, pad_starts[None, None, :], 0), axis=-1)
    dest_flat = dest.reshape(n * TOP_K).astype(I32)
    n_rows = n * TOP_K + N_EXPERTS * tm_e
    nb = n_rows // tm_e
    blk_start = jnp.arange(nb, dtype=I32) * tm_e
    block_e = jnp.minimum(jnp.sum(blk_start[:, None] >= pad_ends[None, :], axis=-1),
                          N_EXPERTS - 1).astype(I32)
    nused = (pad_ends[-1:] // tm_e).astype(I32)
    ids = jnp.where(padded > 0, jnp.arange(N_EXPERTS, dtype=I32), N_EXPERTS)
    later = jnp.concatenate([lax.cummin(ids[::-1])[::-1][1:], jnp.full((1,), N_EXPERTS, I32)])
    next_e = jnp.where(later >= N_EXPERTS, -1, later)[block_e].astype(I32)
    seg_lo = (pad_starts + counts).astype(I32)
    seg_hi = pad_ends.at[N_EXPERTS - 1].set(n_rows).astype(I32)

    blk_cnt = bcnt[::8, :N_EXPERTS]
    blk_base = pad_starts[None, :] + jnp.cumsum(blk_cnt, axis=0) - blk_cnt
    xs = _dispatch(x1, te, blk_base.reshape(-1).astype(I32), blk_cnt.reshape(-1).astype(I32),
                   seg_lo, seg_hi, n_rows, cfg["tm_route"])
    ys = _experts(xs, block_e, nused, next_e, big["w_gu"], big["b_gu"], big["w_down"],
                  big["b_down"], layer, tm_e)
    return _combine(ys, dest_flat, x1, tg, p["ln2_g"].reshape(1, D_MODEL),
                    p["ln2_b"].reshape(1, D_MODEL), cfg["tm_comb"])


def _forward(x, params, cfg):
    b, s, d = x.shape
    h = x.reshape(b * s, d)
    depth = params["w_in"].shape[0]
    big = dict(w_gu=params["w_gu"].reshape(depth * N_EXPERTS, D_MODEL, 2 * D_FF),
               b_gu=params["b_gu"].reshape(depth * N_EXPERTS, 1, 2 * D_FF),
               w_down=params["w_down"].reshape(depth * N_EXPERTS, D_FF, D_MODEL),
               b_down=params["b_down"].reshape(depth * N_EXPERTS, 1, D_MODEL))
    for l in range(depth):
        lam_init = 0.8 - 0.6 * math.exp(-0.3 * l)
        p = {k: v[l] for k, v in params.items() if k not in big}
        h = _layer(h, p, big, l, lam_init, cfg)
    return h.reshape(b, s, d)


def kernel(x, w_in, conv_w, conv_b, gate_b, lam_q1, lam_k1, lam_q2, lam_k2, da_norm_g, w_out,
           ln1_g, ln1_b, w_router, b_router, w_gu, b_gu, w_down, b_down, ln2_g, ln2_b):
    params = dict(w_in=w_in, conv_w=conv_w, conv_b=conv_b, gate_b=gate_b, lam_q1=lam_q1,
                  lam_k1=lam_k1, lam_q2=lam_q2, lam_k2=lam_k2, da_norm_g=da_norm_g, w_out=w_out,
                  ln1_g=ln1_g, ln1_b=ln1_b, w_router=w_router, b_router=b_router, w_gu=w_gu,
                  b_gu=b_gu, w_down=w_down, b_down=b_down, ln2_g=ln2_g, ln2_b=ln2_b)
    cfg = dict(batch=x.shape[0], seq=x.shape[1], tm_proj=512, tm_route=512, tq=2048, tk=512, chunk=256, ml_group=4,
               tm_expert=512, tm_comb=256)
    return _forward(x, params, cfg)
```
